```python
import math
import jax, jax.numpy as jnp
from jax import lax
import numpy as np

D_MODEL = 1024
BATCH = 16
SEQ = 2048
DEPTH = 1
DEC_BATCH = 4
DEC_SEQ = 8192
PAST_LEN = 128

GRID_W = 64
N_HEADS = 8
N_KV_HEADS = 2
HEAD_DIM = 64
Q_PER_KV = N_HEADS // N_KV_HEADS
ATTN_WIDTH = N_HEADS * HEAD_DIM
KV_WIDTH = N_KV_HEADS * HEAD_DIM
AXIS_DIM = HEAD_DIM // 2
ROPE_THETA = 10000.0
Q_BLOCK = 128
HG_HEADS = 4
HG_EXPAND = 128
HG_WIDTH = HG_HEADS * HG_EXPAND
HG_CHUNK = 64
N_MEM = 256
X_HEADS = 4
X_HEAD_DIM = D_MODEL // X_HEADS
D_FF = 4 * D_MODEL
EPS = 1e-6
ALPHA = (2 * DEPTH) ** 0.25
BETA = (8 * DEPTH) ** -0.25
IN_SIZES = (ATTN_WIDTH, KV_WIDTH, KV_WIDTH, HG_WIDTH, HG_WIDTH, HG_WIDTH, HG_WIDTH, HG_WIDTH, D_MODEL, D_MODEL)
IN_WIDTH = ATTN_WIDTH + 2 * KV_WIDTH + 5 * HG_WIDTH + 2 * D_MODEL

kernel_name = 'hybrid_gqa_hgrn2_encoder'


def layer_norm(x, g, b):
    xf = x.astype(jnp.float32)
    mu = jnp.mean(xf, axis=-1, keepdims=True)
    var = jnp.mean(jnp.square(xf - mu), axis=-1, keepdims=True)
    return ((xf - mu) * lax.rsqrt(var + EPS) * g.astype(jnp.float32) + b.astype(jnp.float32)).astype(x.dtype)


def rms_norm(x, g):
    xf = x.astype(jnp.float32)
    return (xf * lax.rsqrt(jnp.mean(xf * xf, axis=-1, keepdims=True) + EPS) * g.astype(jnp.float32)).astype(x.dtype)


def split_columns(h, sizes):
    parts, start = [], 0
    for s in sizes:
        parts.append(h[..., start:start + s])
        start += s
    return parts


def axial_rope_angles(T):
    rows = T // GRID_W
    row = jnp.repeat(jnp.arange(rows, dtype=jnp.float32), GRID_W)
    col = jnp.tile(jnp.arange(GRID_W, dtype=jnp.float32), rows)
    inv_freq = ROPE_THETA ** (-jnp.arange(0, AXIS_DIM, 2, dtype=jnp.float32) / AXIS_DIM)
    return row[:, None] * inv_freq, col[:, None] * inv_freq


def rotate_half(x, ang):
    m = x.shape[-1] // 2
    cos = jnp.cos(ang)[:, None, :]
    sin = jnp.sin(ang)[:, None, :]
    x1 = x[..., :m].astype(jnp.float32)
    x2 = x[..., m:].astype(jnp.float32)
    return jnp.concatenate([x1 * cos - x2 * sin, x1 * sin + x2 * cos], axis=-1)


def apply_axial_rope(x, ang_row, ang_col):
    out = jnp.concatenate([rotate_half(x[..., :AXIS_DIM], ang_row), rotate_half(x[..., AXIS_DIM:], ang_col)], axis=-1)
    return out.astype(x.dtype)


def blocked_gqa_attention(q, k, v):
    B, T = q.shape[:2]
    nb = T // Q_BLOCK
    qb = q.reshape(B, nb, Q_BLOCK, N_KV_HEADS, Q_PER_KV, HEAD_DIM).transpose(1, 0, 3, 4, 2, 5)
    kh = k.transpose(0, 2, 1, 3)
    vh = v.transpose(0, 2, 1, 3)
    scale = 1.0 / math.sqrt(HEAD_DIM)

    def attend(qblk):
        s = jnp.einsum('bkgqd,bksd->bkgqs', qblk, kh).astype(jnp.float32) * scale
        p = jax.nn.softmax(s, axis=-1).astype(vh.dtype)
        return jnp.einsum('bkgqs,bksd->bkgqd', p, vh)

    o = lax.map(attend, qb)
    return o.transpose(1, 0, 4, 2, 3, 5).reshape(B, T, ATTN_WIDTH)


def hgrn2_chunk_scan(q, k, v, log_f):
    B, T, H, dk = q.shape
    dv = v.shape[-1]
    nc = T // HG_CHUNK

    def to_chunks(a):
        return a.reshape(B, nc, HG_CHUNK, H, a.shape[-1]).transpose(1, 0, 3, 2, 4)

    causal = jnp.tril(jnp.ones((HG_CHUNK, HG_CHUNK), dtype=bool))[:, :, None]

    def step(S, inp):
        qc, kc, vc, gc = inp
        b = jnp.cumsum(gc, axis=2)
        o_inter = jnp.einsum('bhtd,bhde->bhte', qc * jnp.exp(b), S)
        rel = jnp.where(causal, b[:, :, :, None, :] - b[:, :, None, :, :], -jnp.inf)
        scores = jnp.einsum('bhtd,bhsd,bhtsd->bhts', qc, kc, jnp.exp(rel))
        o_intra = jnp.einsum('bhts,bhse->bhte', scores, vc)
        b_end = b[:, :, -1:, :]
        S = jnp.exp(b_end[:, :, 0, :, None]) * S + jnp.einsum('bhsd,bhse->bhde', kc * jnp.exp(b_end - b), vc)
        return S, o_inter + o_intra

    S0 = jnp.zeros((B, H, dk, dv), jnp.float32)
    _, o = lax.scan(step, S0, (to_chunks(q), to_chunks(k), to_chunks(v), to_chunks(log_f)))
    return o.transpose(1, 0, 3, 2, 4).reshape(B, T, H, dv)


def token_mixer(x, w_in, w_pa, w_pb, w_out, q_norm, k_norm, lb, g_norm):
    B, T, _ = x.shape
    q_a, k_a, v_a, q_h, zf_fw, zf_bw, i_h, g_h, gate_a, gate_b = split_columns(x @ w_in, IN_SIZES)
    ang_row, ang_col = axial_rope_angles(T)
    q_a = apply_axial_rope(rms_norm(q_a.reshape(B, T, N_HEADS, HEAD_DIM), q_norm), ang_row, ang_col)
    k_a = apply_axial_rope(rms_norm(k_a.reshape(B, T, N_KV_HEADS, HEAD_DIM), k_norm), ang_row, ang_col)
    v_a = v_a.reshape(B, T, N_KV_HEADS, HEAD_DIM)
    o_a = blocked_gqa_attention(q_a, k_a, v_a)
    def heads(a):
        return a.reshape(B, T, HG_HEADS, HG_EXPAND).astype(jnp.float32)
    q_h = jax.nn.silu(heads(q_h))
    i_h = heads(i_h)
    lb = lb.reshape(2, HG_HEADS, HG_EXPAND)
    f_fw = lb[0] + (1.0 - lb[0]) * jax.nn.sigmoid(heads(zf_fw))
    f_bw = lb[1] + (1.0 - lb[1]) * jax.nn.sigmoid(heads(zf_bw))
    o_fw = hgrn2_chunk_scan(q_h, 1.0 - f_fw, i_h, jnp.log(f_fw))
    rev = lambda a: jnp.flip(a, axis=1)
    o_bw = rev(hgrn2_chunk_scan(rev(q_h), rev(1.0 - f_bw), rev(i_h), rev(jnp.log(f_bw))))
    o_h = rms_norm(o_fw + o_bw, g_norm) * jax.nn.silu(heads(g_h))
    o_b = o_h.reshape(B, T, HG_WIDTH).astype(x.dtype)
    merged = jax.nn.sigmoid(gate_a) * (o_a @ w_pa) + jax.nn.sigmoid(gate_b) * (o_b @ w_pb)
    return merged @ w_out


def memory_cross_attention(x, mem, w_q, w_k, w_v, w_o):
    B, T, _ = x.shape
    q = (x @ w_q).reshape(B, T, X_HEADS, X_HEAD_DIM)
    k = (mem @ w_k).reshape(B, N_MEM, X_HEADS, X_HEAD_DIM)
    v = (mem @ w_v).reshape(B, N_MEM, X_HEADS, X_HEAD_DIM)
    s = jnp.einsum('bthd,bmhd->bhtm', q, k).astype(jnp.float32) * (1.0 / math.sqrt(X_HEAD_DIM))
    p = jax.nn.softmax(s, axis=-1).astype(v.dtype)
    o = jnp.einsum('bhtm,bmhd->bthd', p, v).reshape(B, T, D_MODEL)
    return o @ w_o


def squared_relu_mlp(x, w_up, w_down):
    return jnp.square(jax.nn.relu(x @ w_up)) @ w_down


def setup_inputs(seed: int = 0) -> dict:
    key = jax.random.key(seed)
    ks = jax.random.split(key, 24)
    nrm = lambda k, shape: jax.random.normal(k, shape, jnp.float32)
    def w(k, shape, fan_in, scale=1.0):
        return nrm(k, shape) * (scale * fan_in ** -0.5)
    def gain(k, shape):
        return 1.0 + 0.02 * nrm(k, shape)
    def bias(k, shape):
        return 0.02 * nrm(k, shape)
    return {
        'x_prompt': nrm(ks[0], (BATCH, SEQ, D_MODEL)),
        'x_sample': nrm(ks[1], (DEC_BATCH, DEC_SEQ, D_MODEL)),
        'mem_prompt': nrm(ks[2], (BATCH, N_MEM, D_MODEL)),
        'mem_sample': nrm(ks[3], (DEC_BATCH, N_MEM, D_MODEL)),
        'w_in': w(ks[4], (DEPTH, D_MODEL, IN_WIDTH), D_MODEL),
        'w_pa': w(ks[5], (DEPTH, ATTN_WIDTH, D_MODEL), ATTN_WIDTH),
        'w_pb': w(ks[6], (DEPTH, HG_WIDTH, D_MODEL), HG_WIDTH),
        'w_out': w(ks[7], (DEPTH, D_MODEL, D_MODEL), D_MODEL, BETA),
        'q_norm': gain(ks[8], (DEPTH, HEAD_DIM)),
        'k_norm': gain(ks[9], (DEPTH, HEAD_DIM)),
        'hg_lb': 0.1 * nrm(ks[10], (2, DEPTH + 1, HG_WIDTH)),
        'hg_gnorm': gain(ks[11], (DEPTH, HG_EXPAND)),
        'ln1_g': gain(ks[12], (DEPTH, D_MODEL)),
        'ln1_b': bias(ks[13], (DEPTH, D_MODEL)),
        'w_xq': w(ks[14], (DEPTH, D_MODEL, D_MODEL), D_MODEL),
        'w_xk': w(ks[15], (DEPTH, D_MODEL, D_MODEL), D_MODEL),
        'w_xv': w(ks[16], (DEPTH, D_MODEL, D_MODEL), D_MODEL, BETA),
        'w_xo': w(ks[17], (DEPTH, D_MODEL, D_MODEL), D_MODEL, BETA),
        'ln2_g': gain(ks[18], (DEPTH, D_MODEL)),
        'ln2_b': bias(ks[19], (DEPTH, D_MODEL)),
        'w_up': w(ks[20], (DEPTH, D_MODEL, D_FF), D_MODEL),
        'w_down': w(ks[21], (DEPTH, D_FF, D_MODEL), D_FF, BETA),
        'ln3_g': gain(ks[22], (DEPTH, D_MODEL)),
        'ln3_b': bias(ks[23], (DEPTH, D_MODEL)),
    }


def reference(x_prompt, x_sample, mem_prompt, mem_sample, w_in, w_pa, w_pb, w_out, q_norm, k_norm, hg_lb, hg_gnorm, ln1_g, ln1_b, w_xq, w_xk, w_xv, w_xo, ln2_g, ln2_b, w_up, w_down, ln3_g, ln3_b):
    lb_all = jnp.cumsum(jax.nn.softmax(hg_lb.astype(jnp.float32), axis=1), axis=1)

    def run(x, mem):
        for l in range(DEPTH):
            x = layer_norm(ALPHA * x + token_mixer(x, w_in[l], w_pa[l], w_pb[l], w_out[l], q_norm[l], k_norm[l], lb_all[:, l], hg_gnorm[l]), ln1_g[l], ln1_b[l])
            x = layer_norm(ALPHA * x + memory_cross_attention(x, mem, w_xq[l], w_xk[l], w_xv[l], w_xo[l]), ln2_g[l], ln2_b[l])
            x = layer_norm(ALPHA * x + squared_relu_mlp(x, w_up[l], w_down[l]), ln3_g[l], ln3_b[l])
        return x

    y_prompt = run(x_prompt, mem_prompt)
    y_sample = run(x_sample, mem_sample)
    return (y_prompt, y_sample)
```

```python
import functools
import math

import numpy as np
import jax
import jax.numpy as jnp
from jax import lax
from jax.experimental import pallas as pl
from jax.experimental.pallas import tpu as pltpu

F32 = jnp.float32
BF16 = jnp.bfloat16

D_MODEL = 1024
GRID_W = 64
N_HEADS = 8
N_KV_HEADS = 2
HEAD_DIM = 64
ATTN_WIDTH = N_HEADS * HEAD_DIM
KV_WIDTH = N_KV_HEADS * HEAD_DIM
AXIS_DIM = HEAD_DIM // 2
ROPE_THETA = 10000.0
HG_HEADS = 4
HG_EXPAND = 128
HG_WIDTH = HG_HEADS * HG_EXPAND
N_MEM = 256
X_HEADS = 4
X_HEAD_DIM = D_MODEL // X_HEADS
D_FF = 4 * D_MODEL
EPS = 1e-6

LANES = 128
VMEM_LIMIT_V7X = 56 * 1024 * 1024

ROW_TILE = 256
ATTN_TQ = 256
ATTN_TK = 512
HG_CHUNK = 128
NEG_BIG = -1e30


def _cparams(n_axes):
    return pltpu.CompilerParams(
        dimension_semantics=("arbitrary",) * n_axes,
        vmem_limit_bytes=VMEM_LIMIT_V7X,
    )


def _const_spec(shape):
    nd = len(shape)
    return pl.BlockSpec(shape, lambda *_: (0,) * nd, pipeline_mode=pl.Buffered(1))


def _dot(a, b):
    return jnp.dot(a, b, preferred_element_type=F32)


def _dot_nt(a, b):
    return lax.dot_general(a, b, (((1,), (1,)), ((), ())), preferred_element_type=F32)


def _dot_tn(a, b):
    return lax.dot_general(a, b, (((0,), (0,)), ((), ())), preferred_element_type=F32)


def _sigmoid(x):
    return 1.0 / (1.0 + jnp.exp(-x))


def _layer_norm(z, g, b):
    mu = jnp.mean(z, axis=-1, keepdims=True)
    zc = z - mu
    var = jnp.mean(zc * zc, axis=-1, keepdims=True)
    return zc * lax.rsqrt(var + EPS) * g + b


_A_Q0 = 0
_A_K0 = _A_Q0 + ATTN_WIDTH
_A_V0 = _A_K0 + 2 * KV_WIDTH
_A_HG0 = _A_V0 + 2 * KV_WIDTH
_A_GA0 = _A_HG0 + 5 * HG_WIDTH
_A_GB0 = _A_GA0 + D_MODEL
_A_WIDTH = _A_GB0 + D_MODEL
_NORM_CHUNK = 256


def _head_rmsnorm_rope(h, gain, cos, sin, bd, swap_lo):
    sq = h * h
    hi = sq.astype(BF16)
    lo = (sq - hi.astype(F32)).astype(BF16)
    ms = _dot(hi, bd) + _dot(lo, bd)
    hn = h * lax.rsqrt(ms + EPS) * gain
    n = h.shape[1]
    half = AXIS_DIM // 2
    partner = jnp.where(swap_lo, pltpu.roll(hn, n - half, 1), pltpu.roll(hn, half, 1))
    return hn * cos + partner * sin


def _inproj_kernel(x_ref, w_ref, cos_ref, sin_ref, qg_ref, kg_ref, bd_ref,
                   q_out, k_out, v_out, qh_out, zfw_out, zbw_out, ih_out, gh_out, ga_out, gb_out):
    xb = x_ref[...].astype(BF16)
    cos = cos_ref[...]
    sin = sin_ref[...]
    bd = bd_ref[...]
    lane = lax.broadcasted_iota(jnp.int32, (xb.shape[0], _NORM_CHUNK), 1)
    swap_lo = (lane & (AXIS_DIM - 1)) < (AXIS_DIM // 2)

    def proj(c0, n):
        return _dot(xb, w_ref[:, c0:c0 + n])

    scale = 1.0 / math.sqrt(HEAD_DIM)
    for c in range(ATTN_WIDTH // _NORM_CHUNK):
        h = proj(_A_Q0 + c * _NORM_CHUNK, _NORM_CHUNK)
        r = _head_rmsnorm_rope(h, qg_ref[...], cos, sin, bd, swap_lo)
        q_out[:, c * _NORM_CHUNK:(c + 1) * _NORM_CHUNK] = (r * scale).astype(q_out.dtype)
    h = proj(_A_K0, _NORM_CHUNK)
    k_out[...] = _head_rmsnorm_rope(h, kg_ref[...], cos, sin, bd, swap_lo).astype(k_out.dtype)
    v_out[...] = proj(_A_V0, 2 * KV_WIDTH).astype(v_out.dtype)
    for i, o in enumerate((qh_out, zfw_out, zbw_out, ih_out, gh_out)):
        o[...] = proj(_A_HG0 + i * HG_WIDTH, HG_WIDTH)
    ga_out[...] = proj(_A_GA0, D_MODEL)
    gb_out[...] = proj(_A_GB0, D_MODEL)


def _rope_tables(T):
    rows = T // GRID_W
    row = jnp.repeat(jnp.arange(rows, dtype=F32), GRID_W)
    col = jnp.tile(jnp.arange(GRID_W, dtype=F32), rows)
    inv_freq = ROPE_THETA ** (-jnp.arange(0, AXIS_DIM, 2, dtype=F32) / AXIS_DIM)
    ar = row[:, None] * inv_freq
    ac = col[:, None] * inv_freq
    cos = jnp.concatenate([jnp.cos(ar), jnp.cos(ar), jnp.cos(ac), jnp.cos(ac)], axis=1)
    sin = jnp.concatenate([-jnp.sin(ar), jnp.sin(ar), -jnp.sin(ac), jnp.sin(ac)], axis=1)
    reps = _NORM_CHUNK // HEAD_DIM
    return jnp.tile(cos, (1, reps)), jnp.tile(sin, (1, reps))


def _rearrange_w_in(w_in):
    sizes = (ATTN_WIDTH, KV_WIDTH, KV_WIDTH, 5 * HG_WIDTH, 2 * D_MODEL)
    offs = np.cumsum((0,) + sizes)
    wq, wk, wv, whg, wg = (w_in[:, offs[i]:offs[i + 1]] for i in range(5))

    def dup(w):
        heads = [w[:, j * HEAD_DIM:(j + 1) * HEAD_DIM] for j in range(N_KV_HEADS)]
        return jnp.concatenate([h for hd in heads for h in (hd, hd)], axis=1)

    return jnp.concatenate([wq, dup(wk), dup(wv), whg, wg], axis=1).astype(BF16)


def _inproj(x2d, w_r, cos, sin, q_norm, k_norm, T):
    n = x2d.shape[0]
    tm = ROW_TILE
    tpb = T // tm
    reps = _NORM_CHUNK // HEAD_DIM
    qg = jnp.tile(q_norm.astype(F32), reps)[None, :]
    kg = jnp.tile(k_norm.astype(F32), reps)[None, :]
    bd = jnp.asarray(np.kron(np.eye(reps, dtype=np.float32),
                             np.full((HEAD_DIM, HEAD_DIM), 1.0 / HEAD_DIM, np.float32)), BF16)
    row = lambda w: pl.BlockSpec((tm, w), lambda i: (i, 0))
    tab = pl.BlockSpec((tm, _NORM_CHUNK), lambda i: (i % tpb, 0))
    f32o = lambda w: jax.ShapeDtypeStruct((n, w), F32)
    bf16o = lambda w: jax.ShapeDtypeStruct((n, w), BF16)
    return pl.pallas_call(
        _inproj_kernel,
        grid=(n // tm,),
        in_specs=[row(D_MODEL), _const_spec(w_r.shape), tab, tab,
                  _const_spec(qg.shape), _const_spec(kg.shape), _const_spec(bd.shape)],
        out_specs=[row(ATTN_WIDTH), row(2 * KV_WIDTH), row(2 * KV_WIDTH)] + [row(HG_WIDTH)] * 5
                  + [row(D_MODEL)] * 2,
        out_shape=[bf16o(ATTN_WIDTH), bf16o(2 * KV_WIDTH), bf16o(2 * KV_WIDTH)] + [f32o(HG_WIDTH)] * 5
                  + [f32o(D_MODEL)] * 2,
        compiler_params=_cparams(1),
        name="inproj",
    )(x2d, w_r, cos, sin, qg, kg, bd)


def _attn_kernel(q_ref, k_ref, v_ref, o_ref, *, tk):
    tq = q_ref.shape[0]
    nkb = k_ref.shape[0] // tk
    left = lax.broadcasted_iota(jnp.int32, (tq, LANES), 1) < HEAD_DIM
    for c in range(ATTN_WIDTH // LANES):
        kv = (2 * c) // (N_HEADS // N_KV_HEADS)
        qc = q_ref[:, c * LANES:(c + 1) * LANES]
        q_l = jnp.where(left, qc, jnp.zeros_like(qc))
        q_r = jnp.where(left, jnp.zeros_like(qc), qc)

        def body(kb, carry, kv=kv, q_l=q_l, q_r=q_r):
            m_l, l_l, m_r, l_r, acc = carry
            start = pl.multiple_of(kb * tk, tk)
            kblk = k_ref[pl.ds(start, tk), kv * LANES:(kv + 1) * LANES]
            vblk = v_ref[pl.ds(start, tk), kv * LANES:(kv + 1) * LANES]

            def one(qh, m, l):
                s = _dot_nt(qh, kblk)
                m_new = jnp.maximum(m, jnp.max(s, axis=1, keepdims=True))
                alpha = jnp.exp(m - m_new)
                p = jnp.exp(s - m_new)
                l_new = alpha * l + jnp.sum(p, axis=1, keepdims=True)
                return m_new, l_new, alpha, _dot(p.astype(BF16), vblk)

            m_l, l_l, a_l, pv_l = one(q_l, m_l, l_l)
            m_r, l_r, a_r, pv_r = one(q_r, m_r, l_r)
            acc = acc * jnp.where(left, a_l, a_r) + jnp.where(left, pv_l, pv_r)
            return m_l, l_l, m_r, l_r, acc

        neg = jnp.full((tq, 1), NEG_BIG, F32)
        zero = jnp.zeros((tq, 1), F32)
        _, l_l, _, l_r, acc = lax.fori_loop(
            0, nkb, body, (neg, zero, neg, zero, jnp.zeros((tq, LANES), F32)))
        o_ref[:, c * LANES:(c + 1) * LANES] = (acc / jnp.where(left, l_l, l_r)).astype(o_ref.dtype)


def _attention(q, k, v, T):
    n = q.shape[0]
    tq = min(ATTN_TQ, T)
    tk = min(ATTN_TK, T)
    nq = T // tq
    kv_spec = pl.BlockSpec((T, 2 * KV_WIDTH), lambda b, i: (b, 0))
    return pl.pallas_call(
        functools.partial(_attn_kernel, tk=tk),
        grid=(n // T, nq),
        in_specs=[pl.BlockSpec((tq, ATTN_WIDTH), lambda b, i: (b * nq + i, 0)), kv_spec, kv_spec],
        out_specs=pl.BlockSpec((tq, ATTN_WIDTH), lambda b, i: (b * nq + i, 0)),
        out_shape=jax.ShapeDtypeStruct((n, ATTN_WIDTH), BF16),
        compiler_params=_cparams(2),
        name="gqa_attention",
    )(q, k, v)


def _pivot_bcast(p, row, m, pivot):
    c = p.shape[0]
    blk = 2 * m
    if blk >= 8:
        p3 = p.reshape(c // blk, blk, p.shape[1])
        return jnp.broadcast_to(p3[:, pivot:pivot + 1, :], p3.shape).reshape(p.shape)
    res = row & (blk - 1)
    r = p
    for i in range(blk):
        if i != pivot:
            r = jnp.where(res == i, pltpu.roll(p, (i - pivot) % c, 0), r)
    return r


def _hgrn_direction(q, z, v, lb, lvl, st_ref, reverse):
    c = q.shape[0]
    f = lb + (1.0 - lb) * _sigmoid(z)
    g = jnp.log(f)
    k = 1.0 - f
    row = lax.broadcasted_iota(jnp.int32, q.shape, 0)
    p = g
    a = jnp.zeros((c, c), F32)
    for level in range(int(math.log2(c))):
        m = 1 << level
        qside = ((row & m) == 0) if reverse else ((row & m) != 0)
        if level == 0:
            y = jnp.where(qside, q * f, k)
            r = _pivot_bcast(p, row, m, m if reverse else m - 1)
        else:
            r = _pivot_bcast(p, row, m, m if reverse else m - 1)
            y = jnp.exp(jnp.where(qside, p, r - p)) * jnp.where(qside, q, k)
        yb = y.astype(BF16)
        a = jnp.where(lvl == level, _dot_nt(yb, yb), a)
        p = p + jnp.where(qside, r, 0.0)
    diag = jnp.sum(q * k, axis=1, keepdims=True)
    vb = v.astype(BF16)
    st = st_ref[...]
    o = _dot(a.astype(BF16), vb) + diag * v + _dot_nt((q * jnp.exp(p)).astype(BF16), st.astype(BF16))
    tot = p[0:1, :] if reverse else p[c - 1:c, :]
    kd = k * jnp.exp(tot - p)
    st_ref[...] = jnp.exp(tot) * st + _dot_tn(vb, kd.astype(BF16))
    return o


def _hgrn_kernel(qf_ref, zf_ref, vf_ref, qb_ref, zb_ref, vb_ref, lb_ref, lvlf_ref, lvlb_ref,
                 of_ref, ob_ref, st_ref, *, layer):
    @pl.when(pl.program_id(2) == 0)
    def _():
        st_ref[...] = jnp.zeros_like(st_ref)

    raw = lb_ref[...]
    e = jnp.exp(raw - jnp.max(raw, axis=1, keepdims=True))
    lbs = jnp.sum(e[:, :layer + 1, :], axis=1) / jnp.sum(e, axis=1)

    def act(x):
        return x * _sigmoid(x)

    of_ref[...] = _hgrn_direction(act(qf_ref[...]), zf_ref[...], vf_ref[...], lbs[0:1, :],
                                  lvlf_ref[...], st_ref.at[0], False)
    ob_ref[...] = _hgrn_direction(act(qb_ref[...]), zb_ref[...], vb_ref[...], lbs[1:2, :],
                                  lvlb_ref[...], st_ref.at[1], True)


def _level_tables(c):
    t = np.arange(c)[:, None]
    s = np.arange(c)[None, :]
    x = t ^ s
    lv = np.where(x > 0, np.floor(np.log2(np.maximum(x, 1))), -1).astype(np.int32)
    fw = np.where(t > s, lv, -1).astype(np.int32)
    bw = np.where(t < s, lv, -1).astype(np.int32)
    return jnp.asarray(fw), jnp.asarray(bw)


def _hgrn(qh, zfw, zbw, ih, hg_lb, layer, T):
    n = qh.shape[0]
    c = min(HG_CHUNK, T)
    nc = T // c
    lvl_fw, lvl_bw = _level_tables(c)
    fw = pl.BlockSpec((c, HG_EXPAND), lambda b, h, i: (b * nc + i, h))
    bw = pl.BlockSpec((c, HG_EXPAND), lambda b, h, i: (b * nc + nc - 1 - i, h))
    lb_spec = pl.BlockSpec((2, hg_lb.shape[1], HG_EXPAND), lambda b, h, i: (0, 0, h))
    out = jax.ShapeDtypeStruct((n, HG_WIDTH), F32)
    return pl.pallas_call(
        functools.partial(_hgrn_kernel, layer=layer),
        grid=(n // T, HG_HEADS, nc),
        in_specs=[fw, fw, fw, bw, bw, bw, lb_spec, _const_spec((c, c)), _const_spec((c, c))],
        out_specs=[fw, bw],
        out_shape=[out, out],
        scratch_shapes=[pltpu.VMEM((2, HG_EXPAND, HG_EXPAND), F32)],
        compiler_params=_cparams(3),
        name="hgrn2_bidir",
    )(qh, zfw, ih, qh, zbw, ih, hg_lb.astype(F32), lvl_fw, lvl_bw)


def _merge_kernel(x_ref, oa_ref, of_ref, ob_ref, gh_ref, ga_ref, gb_ref,
                  wpa_ref, wpb_ref, wout_ref, gn_ref, lng_ref, lnb_ref, o_ref, *, alpha):
    o = of_ref[...] + ob_ref[...]
    gh = gh_ref[...]
    gn = gn_ref[...]
    parts = []
    for h in range(HG_HEADS):
        oh = o[:, h * HG_EXPAND:(h + 1) * HG_EXPAND]
        ms = jnp.mean(oh * oh, axis=-1, keepdims=True)
        parts.append(oh * lax.rsqrt(ms + EPS) * gn)
    on = jnp.concatenate(parts, axis=1)
    o_b = (on * (gh * _sigmoid(gh))).astype(BF16)
    pa = _dot(oa_ref[...], wpa_ref[...])
    pb = _dot(o_b, wpb_ref[...])
    merged = _sigmoid(ga_ref[...]) * pa + _sigmoid(gb_ref[...]) * pb
    y = _dot(merged.astype(BF16), wout_ref[...])
    o_ref[...] = _layer_norm(alpha * x_ref[...] + y, lng_ref[...], lnb_ref[...])


def _merge(x2d, o_a, o_fw, o_bw, gh, ga, gb, w_pa, w_pb, w_out, g_norm, ln_g, ln_b, alpha):
    n = x2d.shape[0]
    tm = ROW_TILE
    row = lambda w: pl.BlockSpec((tm, w), lambda i: (i, 0))
    vec = lambda a: a.astype(F32)[None, :]
    consts = [w_pa.astype(BF16), w_pb.astype(BF16), w_out.astype(BF16), vec(g_norm), vec(ln_g), vec(ln_b)]
    return pl.pallas_call(
        functools.partial(_merge_kernel, alpha=alpha),
        grid=(n // tm,),
        in_specs=[row(D_MODEL), row(ATTN_WIDTH), row(HG_WIDTH), row(HG_WIDTH), row(HG_WIDTH),
                  row(D_MODEL), row(D_MODEL)] + [_const_spec(a.shape) for a in consts],
        out_specs=row(D_MODEL),
        out_shape=jax.ShapeDtypeStruct((n, D_MODEL), F32),
        compiler_params=_cparams(1),
        name="merge_ln1",
    )(x2d, o_a, o_fw, o_bw, gh, ga, gb, *consts)


def _memkv_kernel(m_ref, wk_ref, wv_ref, k_out, v_out):
    mb = m_ref[...].astype(BF16)
    k_out[...] = _dot(mb, wk_ref[...]).astype(k_out.dtype)
    v_out[...] = _dot(mb, wv_ref[...]).astype(v_out.dtype)


def _memkv(mem2d, w_k, w_v):
    n = mem2d.shape[0]
    tm = N_MEM
    row = pl.BlockSpec((tm, D_MODEL), lambda i: (i, 0))
    out = jax.ShapeDtypeStruct((n, D_MODEL), BF16)
    return pl.pallas_call(
        _memkv_kernel,
        grid=(n // tm,),
        in_specs=[row, _const_spec(w_k.shape), _const_spec(w_v.shape)],
        out_specs=[row, row],
        out_shape=[out, out],
        compiler_params=_cparams(1),
        name="mem_kv",
    )(mem2d, w_k.astype(BF16), w_v.astype(BF16))


def _xattn_kernel(x_ref, k_ref, v_ref, wq_ref, wo_ref, lng_ref, lnb_ref, o_ref, *, alpha):
    x = x_ref[...]
    scale = 1.0 / math.sqrt(X_HEAD_DIM)
    q = (_dot(x.astype(BF16), wq_ref[...]) * scale).astype(BF16)
    outs = []
    for h in range(X_HEADS):
        sl = slice(h * X_HEAD_DIM, (h + 1) * X_HEAD_DIM)
        s = _dot_nt(q[:, sl], k_ref[:, sl])
        p = jnp.exp(s - jnp.max(s, axis=-1, keepdims=True))
        l = jnp.sum(p, axis=-1, keepdims=True)
        outs.append(_dot(p.astype(BF16), v_ref[:, sl]) / l)
    o = jnp.concatenate(outs, axis=1).astype(BF16)
    y = _dot(o, wo_ref[...])
    o_ref[...] = _layer_norm(alpha * x + y, lng_ref[...], lnb_ref[...])


def _xattn(x2d, k_mem, v_mem, w_q, w_o, ln_g, ln_b, alpha, T):
    n = x2d.shape[0]
    tm = ROW_TILE
    tpb = T // tm
    row = pl.BlockSpec((tm, D_MODEL), lambda i: (i, 0))
    mem = pl.BlockSpec((N_MEM, D_MODEL), lambda i: (i // tpb, 0))
    vec = lambda a: a.astype(F32)[None, :]
    consts = [w_q.astype(BF16), w_o.astype(BF16), vec(ln_g), vec(ln_b)]
    return pl.pallas_call(
        functools.partial(_xattn_kernel, alpha=alpha),
        grid=(n // tm,),
        in_specs=[row, mem, mem] + [_const_spec(a.shape) for a in consts],
        out_specs=row,
        out_shape=jax.ShapeDtypeStruct((n, D_MODEL), F32),
        compiler_params=_cparams(1),
        name="mem_xattn_ln2",
    )(x2d, k_mem, v_mem, *consts)


def _mlp_kernel(x_ref, wu_ref, wd_ref, lng_ref, lnb_ref, o_ref, *, alpha):
    x = x_ref[...]
    h = jnp.maximum(_dot(x.astype(BF16), wu_ref[...]), 0.0)
    y = _dot((h * h).astype(BF16), wd_ref[...])
    o_ref[...] = _layer_norm(alpha * x + y, lng_ref[...], lnb_ref[...])


def _mlp(x2d, w_up, w_down, ln_g, ln_b, alpha):
    n = x2d.shape[0]
    tm = ROW_TILE
    row = pl.BlockSpec((tm, D_MODEL), lambda i: (i, 0))
    vec = lambda a: a.astype(F32)[None, :]
    consts = [w_up.astype(BF16), w_down.astype(BF16), vec(ln_g), vec(ln_b)]
    return pl.pallas_call(
        functools.partial(_mlp_kernel, alpha=alpha),
        grid=(n // tm,),
        in_specs=[row] + [_const_spec(a.shape) for a in consts],
        out_specs=row,
        out_shape=jax.ShapeDtypeStruct((n, D_MODEL), F32),
        compiler_params=_cparams(1),
        name="mlp_ln3",
    )(x2d, *consts)


def _run_group(x, mem, p, depth):
    B, T, _ = x.shape
    alpha = (2 * depth) ** 0.25
    x2d = x.reshape(B * T, D_MODEL)
    mem2d = mem.reshape(B * N_MEM, D_MODEL)
    cos, sin = _rope_tables(T)
    for l in range(depth):
        w_r = _rearrange_w_in(p["w_in"][l])
        q, k, v, qh, zfw, zbw, ih, gh, ga, gb = _inproj(x2d, w_r, cos, sin, p["q_norm"][l], p["k_norm"][l], T)
        o_a = _attention(q, k, v, T)
        o_fw, o_bw = _hgrn(qh, zfw, zbw, ih, p["hg_lb"], l, T)
        x2d = _merge(x2d, o_a, o_fw, o_bw, gh, ga, gb, p["w_pa"][l], p["w_pb"][l], p["w_out"][l],
                     p["hg_gnorm"][l], p["ln1_g"][l], p["ln1_b"][l], alpha)
        k_mem, v_mem = _memkv(mem2d, p["w_xk"][l], p["w_xv"][l])
        x2d = _xattn(x2d, k_mem, v_mem, p["w_xq"][l], p["w_xo"][l], p["ln2_g"][l], p["ln2_b"][l], alpha, T)
        x2d = _mlp(x2d, p["w_up"][l], p["w_down"][l], p["ln3_g"][l], p["ln3_b"][l], alpha)
    return x2d.reshape(B, T, D_MODEL)


def kernel(x_prompt, x_sample, mem_prompt, mem_sample, w_in, w_pa, w_pb, w_out, q_norm, k_norm, hg_lb, hg_gnorm, ln1_g, ln1_b, w_xq, w_xk, w_xv, w_xo, ln2_g, ln2_b, w_up, w_down, ln3_g, ln3_b):
    p = dict(w_in=w_in, w_pa=w_pa, w_pb=w_pb, w_out=w_out, q_norm=q_norm, k_norm=k_norm, hg_lb=hg_lb,
             hg_gnorm=hg_gnorm, ln1_g=ln1_g, ln1_b=ln1_b, w_xq=w_xq, w_xk=w_xk, w_xv=w_xv, w_xo=w_xo,
             ln2_g=ln2_g, ln2_b=ln2_b, w_up=w_up, w_down=w_down, ln3_g=ln3_g, ln3_b=ln3_b)
    depth = w_in.shape[0]
    return (_run_group(x_prompt, mem_prompt, p, depth), _run_group(x_sample, mem_sample, p, depth))
```

```python
import functools
import math

import numpy as np
import jax
import jax.numpy as jnp
from jax import lax
from jax.experimental import pallas as pl
from jax.experimental.pallas import tpu as pltpu

F32 = jnp.float32
BF16 = jnp.bfloat16

D_MODEL = 1024
GRID_W = 64
N_HEADS = 8
N_KV_HEADS = 2
HEAD_DIM = 64
ATTN_WIDTH = N_HEADS * HEAD_DIM
KV_WIDTH = N_KV_HEADS * HEAD_DIM
AXIS_DIM = HEAD_DIM // 2
ROPE_THETA = 10000.0
HG_HEADS = 4
HG_EXPAND = 128
HG_WIDTH = HG_HEADS * HG_EXPAND
N_MEM = 256
X_HEADS = 4
X_HEAD_DIM = D_MODEL // X_HEADS
D_FF = 4 * D_MODEL
EPS = 1e-6

LANES = 128
VMEM_LIMIT_V7X = 56 * 1024 * 1024

ROW_TILE = 256
ATTN_TQ = 256
ATTN_TK = 512
HG_CHUNK = 128
NEG_BIG = -1e30


def _cparams(n_axes):
    return pltpu.CompilerParams(
        dimension_semantics=("arbitrary",) * n_axes,
        vmem_limit_bytes=VMEM_LIMIT_V7X,
    )


def _const_spec(shape):
    nd = len(shape)
    return pl.BlockSpec(shape, lambda *_: (0,) * nd, pipeline_mode=pl.Buffered(1))


def _dot(a, b):
    return jnp.dot(a, b, preferred_element_type=F32)


def _dot_nt(a, b):
    return lax.dot_general(a, b, (((1,), (1,)), ((), ())), preferred_element_type=F32)


def _dot_tn(a, b):
    return lax.dot_general(a, b, (((0,), (0,)), ((), ())), preferred_element_type=F32)


def _sigmoid(x):
    return 1.0 / (1.0 + jnp.exp(-x))


def _layer_norm(z, g, b):
    mu = jnp.mean(z, axis=-1, keepdims=True)
    zc = z - mu
    var = jnp.mean(zc * zc, axis=-1, keepdims=True)
    return zc * lax.rsqrt(var + EPS) * g + b


_A_Q0 = 0
_A_K0 = _A_Q0 + ATTN_WIDTH
_A_V0 = _A_K0 + KV_WIDTH
_A_HG0 = _A_V0 + KV_WIDTH
_A_GA0 = _A_HG0 + 5 * HG_WIDTH
_A_GB0 = _A_GA0 + D_MODEL
_NORM_CHUNK = 256
LOG2E = math.log2(math.e)


def _head_rmsnorm_rope(h, gain, cos, sin, bd, swap_lo):
    sq = h * h
    hi = sq.astype(BF16)
    lo = (sq - hi.astype(F32)).astype(BF16)
    ms = _dot(hi, bd) + _dot(lo, bd)
    hn = h * lax.rsqrt(ms + EPS) * gain
    n = h.shape[1]
    half = AXIS_DIM // 2
    partner = jnp.where(swap_lo, pltpu.roll(hn, n - half, 1), pltpu.roll(hn, half, 1))
    return hn * cos + partner * sin


def _inproj_kernel(x_ref, w_ref, cos_ref, sin_ref, qg_ref, kg_ref, bd_ref,
                   qt_out, k_out, vt_out, qh_out, zfw_out, zbw_out, ih_out, gh_out, ga_out, gb_out):
    xb = x_ref[...].astype(BF16)
    cos = cos_ref[...]
    sin = sin_ref[...]
    bd = bd_ref[...]

    def swap_lo(n):
        lane = lax.broadcasted_iota(jnp.int32, (xb.shape[0], n), 1)
        return (lane & (AXIS_DIM - 1)) < (AXIS_DIM // 2)

    def proj(c0, n):
        return _dot(xb, w_ref[:, c0:c0 + n])

    scale = LOG2E / math.sqrt(HEAD_DIM)
    for c in range(ATTN_WIDTH // _NORM_CHUNK):
        h = proj(_A_Q0 + c * _NORM_CHUNK, _NORM_CHUNK)
        r = _head_rmsnorm_rope(h, qg_ref[...], cos, sin, bd, swap_lo(_NORM_CHUNK))
        qt_out[c * _NORM_CHUNK:(c + 1) * _NORM_CHUNK, :] = (r * scale).T.astype(qt_out.dtype)
    kw = KV_WIDTH
    h = proj(_A_K0, kw)
    k_out[...] = _head_rmsnorm_rope(h, kg_ref[...], cos[:, :kw], sin[:, :kw], bd[:kw, :kw],
                                    swap_lo(kw)).astype(k_out.dtype)
    vt_out[...] = proj(_A_V0, kw).T.astype(vt_out.dtype)
    for i, o in enumerate((qh_out, zfw_out, zbw_out, ih_out, gh_out)):
        o[...] = proj(_A_HG0 + i * HG_WIDTH, HG_WIDTH)
    ga_out[...] = proj(_A_GA0, D_MODEL)
    gb_out[...] = proj(_A_GB0, D_MODEL)


def _rope_tables(T):
    rows = T // GRID_W
    row = jnp.repeat(jnp.arange(rows, dtype=F32), GRID_W)
    col = jnp.tile(jnp.arange(GRID_W, dtype=F32), rows)
    inv_freq = ROPE_THETA ** (-jnp.arange(0, AXIS_DIM, 2, dtype=F32) / AXIS_DIM)
    ar = row[:, None] * inv_freq
    ac = col[:, None] * inv_freq
    cos = jnp.concatenate([jnp.cos(ar), jnp.cos(ar), jnp.cos(ac), jnp.cos(ac)], axis=1)
    sin = jnp.concatenate([-jnp.sin(ar), jnp.sin(ar), -jnp.sin(ac), jnp.sin(ac)], axis=1)
    reps = _NORM_CHUNK // HEAD_DIM
    return jnp.tile(cos, (1, reps)), jnp.tile(sin, (1, reps))


def _inproj(x2d, w_b, cos, sin, q_norm, k_norm, T):
    n = x2d.shape[0]
    tm = ROW_TILE
    tpb = T // tm
    reps = _NORM_CHUNK // HEAD_DIM
    qg = jnp.tile(q_norm.astype(F32), reps)[None, :]
    kg = jnp.tile(k_norm.astype(F32), KV_WIDTH // HEAD_DIM)[None, :]
    bd = jnp.asarray(np.kron(np.eye(reps, dtype=np.float32),
                             np.full((HEAD_DIM, HEAD_DIM), 1.0 / HEAD_DIM, np.float32)), BF16)
    row = lambda w: pl.BlockSpec((tm, w), lambda i: (i, 0))
    col = lambda h: pl.BlockSpec((h, tm), lambda i: (0, i))
    tab = pl.BlockSpec((tm, _NORM_CHUNK), lambda i: (i % tpb, 0))
    f32o = lambda w: jax.ShapeDtypeStruct((n, w), F32)
    return pl.pallas_call(
        _inproj_kernel,
        grid=(n // tm,),
        in_specs=[row(D_MODEL), _const_spec(w_b.shape), tab, tab,
                  _const_spec(qg.shape), _const_spec(kg.shape), _const_spec(bd.shape)],
        out_specs=[col(ATTN_WIDTH), row(KV_WIDTH), col(KV_WIDTH)] + [row(HG_WIDTH)] * 5 + [row(D_MODEL)] * 2,
        out_shape=[jax.ShapeDtypeStruct((ATTN_WIDTH, n), BF16), jax.ShapeDtypeStruct((n, KV_WIDTH), BF16),
                   jax.ShapeDtypeStruct((KV_WIDTH, n), BF16)] + [f32o(HG_WIDTH)] * 5 + [f32o(D_MODEL)] * 2,
        compiler_params=_cparams(1),
        name="inproj",
    )(x2d, w_b, cos, sin, qg, kg, bd)


_ATTN_HEADS_PER_LOOP = 2
_ONES_ROWS = 16


def _attn_kernel(qt_ref, k_ref, vt_ref, ot_ref, sa_ref, sb_ref, *, tk):
    tq = qt_ref.shape[1]
    nkb = k_ref.shape[0] // tk
    group = N_HEADS // N_KV_HEADS
    ones = jnp.ones((_ONES_ROWS, tk), BF16)
    zeros = jnp.zeros((HEAD_DIM, tq), BF16)
    for h0 in range(0, N_HEADS, _ATTN_HEADS_PER_LOOP):
        heads = range(h0, h0 + _ATTN_HEADS_PER_LOOP)
        kv = h0 // group
        ws = []
        for h in heads:
            qh = qt_ref[h * HEAD_DIM:(h + 1) * HEAD_DIM, :]
            parts = [zeros] * N_KV_HEADS
            parts[kv] = qh
            ws.append(jnp.concatenate(parts, axis=0))

        def scores(kb, s_ref, ws=ws):
            kblk = k_ref[pl.ds(pl.multiple_of(kb * tk, tk), tk), :]
            for i, w in enumerate(ws):
                s_ref[i] = _dot(kblk, w)

        def process(kb, s_ref, carry, kv=kv):
            start = pl.multiple_of(kb * tk, tk)
            vext = jnp.concatenate([vt_ref[kv * HEAD_DIM:(kv + 1) * HEAD_DIM, pl.ds(start, tk)], ones], axis=0)
            out = []
            for i, (m, acc) in enumerate(carry):
                st = s_ref[i]
                m_new = jnp.maximum(m, jnp.max(st, axis=0, keepdims=True))
                alpha = jnp.exp2(m - m_new)
                pt = jnp.exp2(st - m_new).astype(BF16)
                out.append((m_new, acc * alpha + _dot(vext, pt)))
            return tuple(out)

        def body(i, carry):
            scores(2 * i + 1, sb_ref)
            carry = process(2 * i, sa_ref, carry)
            scores(2 * i + 2, sa_ref)
            return process(2 * i + 1, sb_ref, carry)

        init = tuple((jnp.full((1, tq), NEG_BIG, F32), jnp.zeros((HEAD_DIM + _ONES_ROWS, tq), F32))
                     for _ in heads)
        scores(0, sa_ref)
        carry = lax.fori_loop(0, nkb // 2 - 1, body, init)
        scores(nkb - 1, sb_ref)
        carry = process(nkb - 2, sa_ref, carry)
        carry = process(nkb - 1, sb_ref, carry)
        for h, (_, acc) in zip(heads, carry):
            ot_ref[h * HEAD_DIM:(h + 1) * HEAD_DIM, :] = (
                acc[:HEAD_DIM] / acc[HEAD_DIM:HEAD_DIM + 1]).astype(ot_ref.dtype)


def _attention(qt, k, vt, T):
    n = k.shape[0]
    tq = min(ATTN_TQ, T)
    tk = min(ATTN_TK, T)
    nq = T // tq
    assert (T // tk) % 2 == 0, "key blocks are processed in pairs"
    q_spec = pl.BlockSpec((ATTN_WIDTH, tq), lambda b, i: (0, b * nq + i))
    s_buf = pltpu.VMEM((_ATTN_HEADS_PER_LOOP, tk, tq), F32)
    return pl.pallas_call(
        functools.partial(_attn_kernel, tk=tk),
        grid=(n // T, nq),
        in_specs=[q_spec, pl.BlockSpec((T, KV_WIDTH), lambda b, i: (b, 0)),
                  pl.BlockSpec((KV_WIDTH, T), lambda b, i: (0, b))],
        out_specs=q_spec,
        out_shape=jax.ShapeDtypeStruct((ATTN_WIDTH, n), BF16),
        scratch_shapes=[s_buf, s_buf],
        compiler_params=_cparams(2),
        name="gqa_attention",
    )(qt, k, vt)


def _pivot_bcast(p, row, m, pivot):
    c = p.shape[0]
    blk = 2 * m
    if blk >= 8:
        p3 = p.reshape(c // blk, blk, p.shape[1])
        return jnp.broadcast_to(p3[:, pivot:pivot + 1, :], p3.shape).reshape(p.shape)
    res = row & (blk - 1)
    r = p
    for i in range(blk):
        if i != pivot:
            r = jnp.where(res == i, pltpu.roll(p, (i - pivot) % c, 0), r)
    return r


def _hgrn_direction(q, z, v, lb, lvl, st_ref, reverse):
    c = q.shape[0]
    f = lb + (1.0 - lb) * _sigmoid(z)
    g = jnp.log(f)
    k = 1.0 - f
    row = lax.broadcasted_iota(jnp.int32, q.shape, 0)
    p = g
    a = jnp.zeros((c, c), F32)
    for level in range(int(math.log2(c))):
        m = 1 << level
        qside = ((row & m) == 0) if reverse else ((row & m) != 0)
        if level == 0:
            y = jnp.where(qside, q * f, k)
            r = _pivot_bcast(p, row, m, m if reverse else m - 1)
        else:
            r = _pivot_bcast(p, row, m, m if reverse else m - 1)
            y = jnp.exp(jnp.where(qside, p, r - p)) * jnp.where(qside, q, k)
        yb = y.astype(BF16)
        a = jnp.where(lvl == level, _dot_nt(yb, yb), a)
        p = p + jnp.where(qside, r, 0.0)
    diag = jnp.sum(q * k, axis=1, keepdims=True)
    vb = v.astype(BF16)
    st = st_ref[...]
    o = _dot(a.astype(BF16), vb) + diag * v + _dot_nt((q * jnp.exp(p)).astype(BF16), st.astype(BF16))
    tot = p[0:1, :] if reverse else p[c - 1:c, :]
    kd = k * jnp.exp(tot - p)
    st_ref[...] = jnp.exp(tot) * st + _dot_tn(vb, kd.astype(BF16))
    return o


def _hgrn_kernel(qf_ref, zf_ref, vf_ref, qb_ref, zb_ref, vb_ref, lb_ref, lvlf_ref, lvlb_ref,
                 of_ref, ob_ref, st_ref, *, layer):
    @pl.when(pl.program_id(2) == 0)
    def _():
        st_ref[...] = jnp.zeros_like(st_ref)

    raw = lb_ref[...]
    e = jnp.exp(raw - jnp.max(raw, axis=1, keepdims=True))
    lbs = jnp.sum(e[:, :layer + 1, :], axis=1) / jnp.sum(e, axis=1)

    def act(x):
        return x * _sigmoid(x)

    of_ref[...] = _hgrn_direction(act(qf_ref[...]), zf_ref[...], vf_ref[...], lbs[0:1, :],
                                  lvlf_ref[...], st_ref.at[0], False)
    ob_ref[...] = _hgrn_direction(act(qb_ref[...]), zb_ref[...], vb_ref[...], lbs[1:2, :],
                                  lvlb_ref[...], st_ref.at[1], True)


def _level_tables(c):
    t = np.arange(c)[:, None]
    s = np.arange(c)[None, :]
    x = t ^ s
    lv = np.where(x > 0, np.floor(np.log2(np.maximum(x, 1))), -1).astype(np.int32)
    fw = np.where(t > s, lv, -1).astype(np.int32)
    bw = np.where(t < s, lv, -1).astype(np.int32)
    return jnp.asarray(fw), jnp.asarray(bw)


def _hgrn(qh, zfw, zbw, ih, hg_lb, layer, T):
    n = qh.shape[0]
    c = min(HG_CHUNK, T)
    nc = T // c
    lvl_fw, lvl_bw = _level_tables(c)
    fw = pl.BlockSpec((c, HG_EXPAND), lambda b, h, i: (b * nc + i, h))
    bw = pl.BlockSpec((c, HG_EXPAND), lambda b, h, i: (b * nc + nc - 1 - i, h))
    lb_spec = pl.BlockSpec((2, hg_lb.shape[1], HG_EXPAND), lambda b, h, i: (0, 0, h))
    out = jax.ShapeDtypeStruct((n, HG_WIDTH), F32)
    return pl.pallas_call(
        functools.partial(_hgrn_kernel, layer=layer),
        grid=(n // T, HG_HEADS, nc),
        in_specs=[fw, fw, fw, bw, bw, bw, lb_spec, _const_spec((c, c)), _const_spec((c, c))],
        out_specs=[fw, bw],
        out_shape=[out, out],
        scratch_shapes=[pltpu.VMEM((2, HG_EXPAND, HG_EXPAND), F32)],
        compiler_params=_cparams(3),
        name="hgrn2_bidir",
    )(qh, zfw, ih, qh, zbw, ih, hg_lb.astype(F32), lvl_fw, lvl_bw)


def _merge_kernel(x_ref, oat_ref, of_ref, ob_ref, gh_ref, ga_ref, gb_ref,
                  wpa_ref, wpb_ref, wout_ref, gn_ref, lng_ref, lnb_ref, o_ref, *, alpha):
    o = of_ref[...] + ob_ref[...]
    gh = gh_ref[...]
    gn = gn_ref[...]
    parts = []
    for h in range(HG_HEADS):
        oh = o[:, h * HG_EXPAND:(h + 1) * HG_EXPAND]
        ms = jnp.mean(oh * oh, axis=-1, keepdims=True)
        parts.append(oh * lax.rsqrt(ms + EPS) * gn)
    on = jnp.concatenate(parts, axis=1)
    o_b = (on * (gh * _sigmoid(gh))).astype(BF16)
    pa = _dot_tn(oat_ref[...], wpa_ref[...])
    pb = _dot(o_b, wpb_ref[...])
    merged = _sigmoid(ga_ref[...]) * pa + _sigmoid(gb_ref[...]) * pb
    y = _dot(merged.astype(BF16), wout_ref[...])
    o_ref[...] = _layer_norm(alpha * x_ref[...] + y, lng_ref[...], lnb_ref[...])


def _merge(x2d, o_at, o_fw, o_bw, gh, ga, gb, w_pa, w_pb, w_out, g_norm, ln_g, ln_b, alpha):
    n = x2d.shape[0]
    tm = ROW_TILE
    row = lambda w: pl.BlockSpec((tm, w), lambda i: (i, 0))
    vec = lambda a: a.astype(F32)[None, :]
    consts = [w_pa.astype(BF16), w_pb.astype(BF16), w_out.astype(BF16), vec(g_norm), vec(ln_g), vec(ln_b)]
    return pl.pallas_call(
        functools.partial(_merge_kernel, alpha=alpha),
        grid=(n // tm,),
        in_specs=[row(D_MODEL), pl.BlockSpec((ATTN_WIDTH, tm), lambda i: (0, i)),
                  row(HG_WIDTH), row(HG_WIDTH), row(HG_WIDTH),
                  row(D_MODEL), row(D_MODEL)] + [_const_spec(a.shape) for a in consts],
        out_specs=row(D_MODEL),
        out_shape=jax.ShapeDtypeStruct((n, D_MODEL), F32),
        compiler_params=_cparams(1),
        name="merge_ln1",
    )(x2d, o_at, o_fw, o_bw, gh, ga, gb, *consts)


def _memkv_kernel(m_ref, wk_ref, wv_ref, k_out, v_out):
    mb = m_ref[...].astype(BF16)
    k_out[...] = _dot(mb, wk_ref[...]).astype(k_out.dtype)
    v_out[...] = _dot(mb, wv_ref[...]).astype(v_out.dtype)


def _memkv(mem2d, w_k, w_v):
    n = mem2d.shape[0]
    tm = N_MEM
    row = pl.BlockSpec((tm, D_MODEL), lambda i: (i, 0))
    out = jax.ShapeDtypeStruct((n, D_MODEL), BF16)
    return pl.pallas_call(
        _memkv_kernel,
        grid=(n // tm,),
        in_specs=[row, _const_spec(w_k.shape), _const_spec(w_v.shape)],
        out_specs=[row, row],
        out_shape=[out, out],
        compiler_params=_cparams(1),
        name="mem_kv",
    )(mem2d, w_k.astype(BF16), w_v.astype(BF16))


def _xattn_kernel(x_ref, k_ref, v_ref, wq_ref, wo_ref, lng_ref, lnb_ref, o_ref, *, alpha):
    x = x_ref[...]
    scale = 1.0 / math.sqrt(X_HEAD_DIM)
    q = (_dot(x.astype(BF16), wq_ref[...]) * scale).astype(BF16)
    outs = []
    for h in range(X_HEADS):
        sl = slice(h * X_HEAD_DIM, (h + 1) * X_HEAD_DIM)
        s = _dot_nt(q[:, sl], k_ref[:, sl])
        p = jnp.exp(s - jnp.max(s, axis=-1, keepdims=True))
        l = jnp.sum(p, axis=-1, keepdims=True)
        outs.append(_dot(p.astype(BF16), v_ref[:, sl]) / l)
    o = jnp.concatenate(outs, axis=1).astype(BF16)
    y = _dot(o, wo_ref[...])
    o_ref[...] = _layer_norm(alpha * x + y, lng_ref[...], lnb_ref[...])


def _xattn(x2d, k_mem, v_mem, w_q, w_o, ln_g, ln_b, alpha, T):
    n = x2d.shape[0]
    tm = ROW_TILE
    tpb = T // tm
    row = pl.BlockSpec((tm, D_MODEL), lambda i: (i, 0))
    mem = pl.BlockSpec((N_MEM, D_MODEL), lambda i: (i // tpb, 0))
    vec = lambda a: a.astype(F32)[None, :]
    consts = [w_q.astype(BF16), w_o.astype(BF16), vec(ln_g), vec(ln_b)]
    return pl.pallas_call(
        functools.partial(_xattn_kernel, alpha=alpha),
        grid=(n // tm,),
        in_specs=[row, mem, mem] + [_const_spec(a.shape) for a in consts],
        out_specs=row,
        out_shape=jax.ShapeDtypeStruct((n, D_MODEL), F32),
        compiler_params=_cparams(1),
        name="mem_xattn_ln2",
    )(x2d, k_mem, v_mem, *consts)


def _mlp_kernel(x_ref, wu_ref, wd_ref, lng_ref, lnb_ref, o_ref, *, alpha):
    x = x_ref[...]
    h = jnp.maximum(_dot(x.astype(BF16), wu_ref[...]), 0.0)
    y = _dot((h * h).astype(BF16), wd_ref[...])
    o_ref[...] = _layer_norm(alpha * x + y, lng_ref[...], lnb_ref[...])


def _mlp(x2d, w_up, w_down, ln_g, ln_b, alpha):
    n = x2d.shape[0]
    tm = ROW_TILE
    row = pl.BlockSpec((tm, D_MODEL), lambda i: (i, 0))
    vec = lambda a: a.astype(F32)[None, :]
    consts = [w_up.astype(BF16), w_down.astype(BF16), vec(ln_g), vec(ln_b)]
    return pl.pallas_call(
        functools.partial(_mlp_kernel, alpha=alpha),
        grid=(n // tm,),
        in_specs=[row] + [_const_spec(a.shape) for a in consts],
        out_specs=row,
        out_shape=jax.ShapeDtypeStruct((n, D_MODEL), F32),
        compiler_params=_cparams(1),
        name="mlp_ln3",
    )(x2d, *consts)


def _run_group(x, mem, p, depth):
    B, T, _ = x.shape
    alpha = (2 * depth) ** 0.25
    x2d = x.reshape(B * T, D_MODEL)
    mem2d = mem.reshape(B * N_MEM, D_MODEL)
    cos, sin = _rope_tables(T)
    for l in range(depth):
        qt, k, vt, qh, zfw, zbw, ih, gh, ga, gb = _inproj(
            x2d, p["w_in"][l].astype(BF16), cos, sin, p["q_norm"][l], p["k_norm"][l], T)
        o_at = _attention(qt, k, vt, T)
        o_fw, o_bw = _hgrn(qh, zfw, zbw, ih, p["hg_lb"], l, T)
        x2d = _merge(x2d, o_at, o_fw, o_bw, gh, ga, gb, p["w_pa"][l], p["w_pb"][l], p["w_out"][l],
                     p["hg_gnorm"][l], p["ln1_g"][l], p["ln1_b"][l], alpha)
        k_mem, v_mem = _memkv(mem2d, p["w_xk"][l], p["w_xv"][l])
        x2d = _xattn(x2d, k_mem, v_mem, p["w_xq"][l], p["w_xo"][l], p["ln2_g"][l], p["ln2_b"][l], alpha, T)
        x2d = _mlp(x2d, p["w_up"][l], p["w_down"][l], p["ln3_g"][l], p["ln3_b"][l], alpha)
    return x2d.reshape(B, T, D_MODEL)


def kernel(x_prompt, x_sample, mem_prompt, mem_sample, w_in, w_pa, w_pb, w_out, q_norm, k_norm, hg_lb, hg_gnorm, ln1_g, ln1_b, w_xq, w_xk, w_xv, w_xo, ln2_g, ln2_b, w_up, w_down, ln3_g, ln3_b):
    p = dict(w_in=w_in, w_pa=w_pa, w_pb=w_pb, w_out=w_out, q_norm=q_norm, k_norm=k_norm, hg_lb=hg_lb,
             hg_gnorm=hg_gnorm, ln1_g=ln1_g, ln1_b=ln1_b, w_xq=w_xq, w_xk=w_xk, w_xv=w_xv, w_xo=w_xo,
             ln2_g=ln2_g, ln2_b=ln2_b, w_up=w_up, w_down=w_down, ln3_g=ln3_g, ln3_b=ln3_b)
    depth = w_in.shape[0]
    return (_run_group(x_prompt, mem_prompt, p, depth), _run_group(x_sample, mem_sample, p, depth))
```

```python
import functools
import math

import numpy as np
import jax
import jax.numpy as jnp
from jax import lax
from jax.experimental import pallas as pl
from jax.experimental.pallas import tpu as pltpu

F32 = jnp.float32
BF16 = jnp.bfloat16

D_MODEL = 1024
GRID_W = 64
N_HEADS = 8
N_KV_HEADS = 2
HEAD_DIM = 64
ATTN_WIDTH = N_HEADS * HEAD_DIM
KV_WIDTH = N_KV_HEADS * HEAD_DIM
AXIS_DIM = HEAD_DIM // 2
ROPE_THETA = 10000.0
HG_HEADS = 4
HG_EXPAND = 128
HG_WIDTH = HG_HEADS * HG_EXPAND
N_MEM = 256
X_HEADS = 4
X_HEAD_DIM = D_MODEL // X_HEADS
D_FF = 4 * D_MODEL
EPS = 1e-6

LANES = 128
VMEM_LIMIT_V7X = 56 * 1024 * 1024

ROW_TILE = 256
ATTN_TQ = 256
ATTN_TK = 1024
HG_CHUNK = 128
NEG_BIG = -1e30


def _cparams(n_axes):
    return pltpu.CompilerParams(
        dimension_semantics=("arbitrary",) * n_axes,
        vmem_limit_bytes=VMEM_LIMIT_V7X,
    )


def _const_spec(shape):
    nd = len(shape)
    return pl.BlockSpec(shape, lambda *_: (0,) * nd, pipeline_mode=pl.Buffered(1))


def _dot(a, b):
    return jnp.dot(a, b, preferred_element_type=F32)


def _dot_nt(a, b):
    return lax.dot_general(a, b, (((1,), (1,)), ((), ())), preferred_element_type=F32)


def _dot_tn(a, b):
    return lax.dot_general(a, b, (((0,), (0,)), ((), ())), preferred_element_type=F32)


def _sigmoid(x):
    return 0.5 * jnp.tanh(0.5 * x) + 0.5


def _layer_norm(z, g, b):
    mu = jnp.mean(z, axis=-1, keepdims=True)
    zc = z - mu
    var = jnp.mean(zc * zc, axis=-1, keepdims=True)
    return zc * lax.rsqrt(var + EPS) * g + b


_A_Q0 = 0
_A_K0 = _A_Q0 + ATTN_WIDTH
_A_V0 = _A_K0 + KV_WIDTH
_A_HG0 = _A_V0 + KV_WIDTH
_A_GA0 = _A_HG0 + 5 * HG_WIDTH
_A_GB0 = _A_GA0 + D_MODEL
_NORM_CHUNK = 256
LOG2E = math.log2(math.e)


def _head_rmsnorm_rope(h, gain, cos, sin, bd, swap_lo):
    sq = h * h
    hi = sq.astype(BF16)
    lo = (sq - hi.astype(F32)).astype(BF16)
    ms = _dot(hi, bd) + _dot(lo, bd)
    hn = h * lax.rsqrt(ms + EPS) * gain
    n = h.shape[1]
    half = AXIS_DIM // 2
    partner = jnp.where(swap_lo, pltpu.roll(hn, n - half, 1), pltpu.roll(hn, half, 1))
    return hn * cos + partner * sin


def _inproj_kernel(x_ref, w_ref, cos_ref, sin_ref, qg_ref, kg_ref, bd_ref, lb_ref,
                   qt_out, k_out, vt_out, qh_out, gfw_out, gbw_out, ih_out, gh_out, ga_out, gb_out, *, layer):
    xb = x_ref[...].astype(BF16)
    cos = cos_ref[...]
    sin = sin_ref[...]
    bd = bd_ref[...]

    def swap_lo(n):
        lane = lax.broadcasted_iota(jnp.int32, (xb.shape[0], n), 1)
        return (lane & (AXIS_DIM - 1)) < (AXIS_DIM // 2)

    def proj(c0, n):
        return _dot(xb, w_ref[:, c0:c0 + n])

    scale = LOG2E / math.sqrt(HEAD_DIM)
    for c in range(ATTN_WIDTH // _NORM_CHUNK):
        h = proj(_A_Q0 + c * _NORM_CHUNK, _NORM_CHUNK)
        r = _head_rmsnorm_rope(h, qg_ref[...], cos, sin, bd, swap_lo(_NORM_CHUNK))
        qt_out[c * _NORM_CHUNK:(c + 1) * _NORM_CHUNK, :] = (r * scale).T.astype(qt_out.dtype)
    kw = KV_WIDTH
    h = proj(_A_K0, kw)
    k_out[...] = _head_rmsnorm_rope(h, kg_ref[...], cos[:, :kw], sin[:, :kw], bd[:kw, :kw],
                                    swap_lo(kw)).astype(k_out.dtype)
    vt_out[...] = proj(_A_V0, kw).T.astype(vt_out.dtype)

    def hg(i):
        return proj(_A_HG0 + i * HG_WIDTH, HG_WIDTH)

    raw = lb_ref[...]
    e = jnp.exp(raw - jnp.max(raw, axis=1, keepdims=True))
    lbs = jnp.sum(e[:, :layer + 1, :], axis=1) / jnp.sum(e, axis=1)
    qh = hg(0)
    qh_out[...] = qh * _sigmoid(qh)
    for d, o in enumerate((gfw_out, gbw_out)):
        lb = lbs[d:d + 1, :]
        o[...] = jnp.log(lb + (1.0 - lb) * _sigmoid(hg(1 + d)))
    ih_out[...] = hg(3).astype(ih_out.dtype)
    gh = hg(4)
    gh_out[...] = (gh * _sigmoid(gh)).astype(gh_out.dtype)
    ga_out[...] = _sigmoid(proj(_A_GA0, D_MODEL)).astype(ga_out.dtype)
    gb_out[...] = _sigmoid(proj(_A_GB0, D_MODEL)).astype(gb_out.dtype)


def _rope_tables(T):
    rows = T // GRID_W
    row = jnp.repeat(jnp.arange(rows, dtype=F32), GRID_W)
    col = jnp.tile(jnp.arange(GRID_W, dtype=F32), rows)
    inv_freq = ROPE_THETA ** (-jnp.arange(0, AXIS_DIM, 2, dtype=F32) / AXIS_DIM)
    ar = row[:, None] * inv_freq
    ac = col[:, None] * inv_freq
    cos = jnp.concatenate([jnp.cos(ar), jnp.cos(ar), jnp.cos(ac), jnp.cos(ac)], axis=1)
    sin = jnp.concatenate([-jnp.sin(ar), jnp.sin(ar), -jnp.sin(ac), jnp.sin(ac)], axis=1)
    reps = _NORM_CHUNK // HEAD_DIM
    return jnp.tile(cos, (1, reps)), jnp.tile(sin, (1, reps))


def _inproj(x2d, w_b, cos, sin, q_norm, k_norm, hg_lb, layer, T):
    n = x2d.shape[0]
    lb = hg_lb.astype(F32)
    tm = ROW_TILE
    tpb = T // tm
    reps = _NORM_CHUNK // HEAD_DIM
    qg = jnp.tile(q_norm.astype(F32), reps)[None, :]
    kg = jnp.tile(k_norm.astype(F32), KV_WIDTH // HEAD_DIM)[None, :]
    bd = jnp.asarray(np.kron(np.eye(reps, dtype=np.float32),
                             np.full((HEAD_DIM, HEAD_DIM), 1.0 / HEAD_DIM, np.float32)), BF16)
    row = lambda w: pl.BlockSpec((tm, w), lambda i: (i, 0))
    col = lambda h: pl.BlockSpec((h, tm), lambda i: (0, i))
    tab = pl.BlockSpec((tm, _NORM_CHUNK), lambda i: (i % tpb, 0))
    out = lambda w, dt: jax.ShapeDtypeStruct((n, w), dt)
    return pl.pallas_call(
        functools.partial(_inproj_kernel, layer=layer),
        grid=(n // tm,),
        in_specs=[row(D_MODEL), _const_spec(w_b.shape), tab, tab,
                  _const_spec(qg.shape), _const_spec(kg.shape), _const_spec(bd.shape), _const_spec(lb.shape)],
        out_specs=[col(ATTN_WIDTH), row(KV_WIDTH), col(KV_WIDTH)] + [row(HG_WIDTH)] * 5 + [row(D_MODEL)] * 2,
        out_shape=[jax.ShapeDtypeStruct((ATTN_WIDTH, n), BF16), jax.ShapeDtypeStruct((n, KV_WIDTH), BF16),
                   jax.ShapeDtypeStruct((KV_WIDTH, n), BF16)]
                  + [out(HG_WIDTH, dt) for dt in (F32, F32, F32, BF16, BF16)] + [out(D_MODEL, BF16)] * 2,
        compiler_params=_cparams(1),
        name="inproj",
    )(x2d, w_b, cos, sin, qg, kg, bd, lb)


_ATTN_HEADS_PER_LOOP = 2
_ONES_ROWS = 16


def _attn_kernel(qt_ref, k_ref, vt_ref, ot_ref, sa_ref, sb_ref, *, tk):
    tq = qt_ref.shape[1]
    nkb = k_ref.shape[0] // tk
    group = N_HEADS // N_KV_HEADS
    ones = jnp.ones((_ONES_ROWS, tk), BF16)
    zeros = jnp.zeros((HEAD_DIM, tq), BF16)
    for h0 in range(0, N_HEADS, _ATTN_HEADS_PER_LOOP):
        heads = range(h0, h0 + _ATTN_HEADS_PER_LOOP)
        kv = h0 // group
        ws = []
        for h in heads:
            qh = qt_ref[h * HEAD_DIM:(h + 1) * HEAD_DIM, :]
            parts = [zeros] * N_KV_HEADS
            parts[kv] = qh
            ws.append(jnp.concatenate(parts, axis=0))

        def scores(kb, s_ref, ws=ws):
            kblk = k_ref[pl.ds(pl.multiple_of(kb * tk, tk), tk), :]
            for i, w in enumerate(ws):
                s_ref[i] = _dot(kblk, w)

        def process(kb, s_ref, carry, kv=kv):
            start = pl.multiple_of(kb * tk, tk)
            vext = jnp.concatenate([vt_ref[kv * HEAD_DIM:(kv + 1) * HEAD_DIM, pl.ds(start, tk)], ones], axis=0)
            out = []
            for i, (m, acc) in enumerate(carry):
                st = s_ref[i]
                m_new = jnp.maximum(m, jnp.max(st, axis=0, keepdims=True))
                alpha = jnp.exp2(m - m_new)
                pt = jnp.exp2(st - m_new).astype(BF16)
                out.append((m_new, acc * alpha + _dot(vext, pt)))
            return tuple(out)

        def body(i, carry):
            scores(2 * i + 1, sb_ref)
            carry = process(2 * i, sa_ref, carry)
            scores(2 * i + 2, sa_ref)
            return process(2 * i + 1, sb_ref, carry)

        init = tuple((jnp.full((1, tq), NEG_BIG, F32), jnp.zeros((HEAD_DIM + _ONES_ROWS, tq), F32))
                     for _ in heads)
        scores(0, sa_ref)
        carry = lax.fori_loop(0, nkb // 2 - 1, body, init)
        scores(nkb - 1, sb_ref)
        carry = process(nkb - 2, sa_ref, carry)
        carry = process(nkb - 1, sb_ref, carry)
        for h, (_, acc) in zip(heads, carry):
            ot_ref[h * HEAD_DIM:(h + 1) * HEAD_DIM, :] = (
                acc[:HEAD_DIM] / acc[HEAD_DIM:HEAD_DIM + 1]).astype(ot_ref.dtype)


def _attention(qt, k, vt, T):
    n = k.shape[0]
    tq = min(ATTN_TQ, T)
    tk = min(ATTN_TK, T)
    nq = T // tq
    assert (T // tk) % 2 == 0, "key blocks are processed in pairs"
    q_spec = pl.BlockSpec((ATTN_WIDTH, tq), lambda b, i: (0, b * nq + i))
    s_buf = pltpu.VMEM((_ATTN_HEADS_PER_LOOP, tk, tq), F32)
    return pl.pallas_call(
        functools.partial(_attn_kernel, tk=tk),
        grid=(n // T, nq),
        in_specs=[q_spec, pl.BlockSpec((T, KV_WIDTH), lambda b, i: (b, 0)),
                  pl.BlockSpec((KV_WIDTH, T), lambda b, i: (0, b))],
        out_specs=q_spec,
        out_shape=jax.ShapeDtypeStruct((ATTN_WIDTH, n), BF16),
        scratch_shapes=[s_buf, s_buf],
        compiler_params=_cparams(2),
        name="gqa_attention",
    )(qt, k, vt)


SUBLANES = 8


def _pivot_bcast(p, row, m, pivot):
    c, dk = p.shape
    blk = 2 * m
    p3 = p.reshape(c // SUBLANES, SUBLANES, dk)
    if blk == SUBLANES:
        return jnp.broadcast_to(p3[:, pivot:pivot + 1, :], p3.shape).reshape(c, dk)
    row3 = row.reshape(p3.shape)
    for j in range(int(math.log2(blk))):
        step = 1 << j
        if (pivot >> j) & 1:
            p3 = jnp.where((row3 & step) == 0, pltpu.roll(p3, SUBLANES - step, 1), p3)
        else:
            p3 = jnp.where((row3 & step) != 0, pltpu.roll(p3, step, 1), p3)
    return p3.reshape(c, dk)


_HG_HEADS_INTERLEAVED = 4
_DONE = object()


def _hgrn_direction(q, g, v, lvl_ref, st_ref, o_ref, reverse):
    c = q.shape[0]
    f = jnp.exp(g)
    k = 1.0 - f
    row = lax.broadcasted_iota(jnp.int32, q.shape, 0)
    p = g
    a = jnp.zeros((c, c), F32)
    for level in range(int(math.log2(c))):
        m = 1 << level
        if m < SUBLANES:
            qside = ((row & m) == 0) if reverse else ((row & m) != 0)
            r = _pivot_bcast(p, row, m, m if reverse else m - 1)
            if level == 0:
                y = jnp.where(qside, q * f, k)
            else:
                y = jnp.exp(jnp.where(qside, p, r - p)) * jnp.where(qside, q, k)
            p = p + jnp.where(qside, r, 0.0)
        else:
            ys, ps = [], []
            for b in range(0, c, 2 * m):
                lo, hi = slice(b, b + m), slice(b + m, b + 2 * m)
                src, qry = (hi, lo) if reverse else (lo, hi)
                piv = b + m if reverse else b + m - 1
                r = p[piv:piv + 1, :]
                y_src = jnp.exp(r - p[src]) * k[src]
                y_qry = jnp.exp(p[qry]) * q[qry]
                p_qry = p[qry] + r
                ys += [y_qry, y_src] if reverse else [y_src, y_qry]
                ps += [p_qry, p[src]] if reverse else [p[src], p_qry]
            y = jnp.concatenate(ys, axis=0)
            p = jnp.concatenate(ps, axis=0)
        yb = y.astype(BF16)
        scores = _dot_nt(yb, yb)
        yield
        a = jnp.where(lvl_ref[...] == level, scores, a)
    diag = jnp.sum(q * k, axis=1, keepdims=True)
    st = st_ref[...]
    intra = _dot(a.astype(BF16), v)
    inter = _dot_nt((q * jnp.exp(p)).astype(BF16), st.astype(BF16))
    tot = p[0:1, :] if reverse else p[c - 1:c, :]
    kd = k * jnp.exp(tot - p)
    update = _dot_tn(v, kd.astype(BF16))
    yield
    o_ref[...] = intra + diag * v.astype(F32) + inter
    st_ref[...] = jnp.exp(tot) * st + update


def _hgrn_kernel(qf_ref, gf_ref, vf_ref, qb_ref, gb_ref, vb_ref, lvlf_ref, lvlb_ref,
                 of_ref, ob_ref, st_ref):
    @pl.when(pl.program_id(1) == 0)
    def _():
        st_ref[...] = jnp.zeros_like(st_ref)

    for h0 in range(0, HG_HEADS, _HG_HEADS_INTERLEAVED):
        chunks = []
        for h in range(h0, h0 + _HG_HEADS_INTERLEAVED):
            sl = slice(h * HG_EXPAND, (h + 1) * HG_EXPAND)
            chunks.append(_hgrn_direction(qf_ref[:, sl], gf_ref[:, sl], vf_ref[:, sl], lvlf_ref,
                                          st_ref.at[0, h], of_ref.at[:, sl], False))
            chunks.append(_hgrn_direction(qb_ref[:, sl], gb_ref[:, sl], vb_ref[:, sl], lvlb_ref,
                                          st_ref.at[1, h], ob_ref.at[:, sl], True))
        while chunks:
            chunks = [ch for ch in chunks if next(ch, _DONE) is not _DONE]


def _level_tables(c):
    t = np.arange(c)[:, None]
    s = np.arange(c)[None, :]
    x = t ^ s
    lv = np.where(x > 0, np.floor(np.log2(np.maximum(x, 1))), -1).astype(np.int32)
    fw = np.where(t > s, lv, -1).astype(np.int32)
    bw = np.where(t < s, lv, -1).astype(np.int32)
    return jnp.asarray(fw), jnp.asarray(bw)


def _hgrn(qh, gfw, gbw, ih, T):
    n = qh.shape[0]
    c = min(HG_CHUNK, T)
    nc = T // c
    lvl_fw, lvl_bw = _level_tables(c)
    fw = pl.BlockSpec((c, HG_WIDTH), lambda b, i: (b * nc + i, 0))
    bw = pl.BlockSpec((c, HG_WIDTH), lambda b, i: (b * nc + nc - 1 - i, 0))
    out = jax.ShapeDtypeStruct((n, HG_WIDTH), F32)
    return pl.pallas_call(
        _hgrn_kernel,
        grid=(n // T, nc),
        in_specs=[fw, fw, fw, bw, bw, bw, _const_spec((c, c)), _const_spec((c, c))],
        out_specs=[fw, bw],
        out_shape=[out, out],
        scratch_shapes=[pltpu.VMEM((2, HG_HEADS, HG_EXPAND, HG_EXPAND), F32)],
        compiler_params=_cparams(2),
        name="hgrn2_bidir",
    )(qh, gfw, ih, qh, gbw, ih, lvl_fw, lvl_bw)


def _merge_kernel(x_ref, oat_ref, of_ref, ob_ref, gh_ref, ga_ref, gb_ref,
                  wpa_ref, wpb_ref, wout_ref, gn_ref, lng_ref, lnb_ref, o_ref, *, alpha):
    o = of_ref[...] + ob_ref[...]
    gn = gn_ref[...]
    parts = []
    for h in range(HG_HEADS):
        oh = o[:, h * HG_EXPAND:(h + 1) * HG_EXPAND]
        ms = jnp.mean(oh * oh, axis=-1, keepdims=True)
        parts.append(oh * lax.rsqrt(ms + EPS) * gn)
    on = jnp.concatenate(parts, axis=1)
    o_b = (on * gh_ref[...].astype(F32)).astype(BF16)
    pa = _dot_tn(oat_ref[...], wpa_ref[...])
    pb = _dot(o_b, wpb_ref[...])
    merged = ga_ref[...].astype(F32) * pa + gb_ref[...].astype(F32) * pb
    y = _dot(merged.astype(BF16), wout_ref[...])
    o_ref[...] = _layer_norm(alpha * x_ref[...] + y, lng_ref[...], lnb_ref[...])


def _merge(x2d, o_at, o_fw, o_bw, gh, ga, gb, w_pa, w_pb, w_out, g_norm, ln_g, ln_b, alpha):
    n = x2d.shape[0]
    tm = ROW_TILE
    row = lambda w: pl.BlockSpec((tm, w), lambda i: (i, 0))
    vec = lambda a: a.astype(F32)[None, :]
    consts = [w_pa.astype(BF16), w_pb.astype(BF16), w_out.astype(BF16), vec(g_norm), vec(ln_g), vec(ln_b)]
    return pl.pallas_call(
        functools.partial(_merge_kernel, alpha=alpha),
        grid=(n // tm,),
        in_specs=[row(D_MODEL), pl.BlockSpec((ATTN_WIDTH, tm), lambda i: (0, i)),
                  row(HG_WIDTH), row(HG_WIDTH), row(HG_WIDTH),
                  row(D_MODEL), row(D_MODEL)] + [_const_spec(a.shape) for a in consts],
        out_specs=row(D_MODEL),
        out_shape=jax.ShapeDtypeStruct((n, D_MODEL), F32),
        compiler_params=_cparams(1),
        name="merge_ln1",
    )(x2d, o_at, o_fw, o_bw, gh, ga, gb, *consts)


def _memkv_kernel(m_ref, wk_ref, wv_ref, k_out, v_out):
    mb = m_ref[...].astype(BF16)
    k_out[...] = _dot(mb, wk_ref[...]).astype(k_out.dtype)
    v_out[...] = _dot(mb, wv_ref[...]).astype(v_out.dtype)


def _memkv(mem2d, w_k, w_v):
    n = mem2d.shape[0]
    tm = N_MEM
    row = pl.BlockSpec((tm, D_MODEL), lambda i: (i, 0))
    out = jax.ShapeDtypeStruct((n, D_MODEL), BF16)
    return pl.pallas_call(
        _memkv_kernel,
        grid=(n // tm,),
        in_specs=[row, _const_spec(w_k.shape), _const_spec(w_v.shape)],
        out_specs=[row, row],
        out_shape=[out, out],
        compiler_params=_cparams(1),
        name="mem_kv",
    )(mem2d, w_k.astype(BF16), w_v.astype(BF16))


def _xattn_kernel(x_ref, k_ref, v_ref, wq_ref, wo_ref, lng_ref, lnb_ref, o_ref, *, alpha):
    x = x_ref[...]
    scale = 1.0 / math.sqrt(X_HEAD_DIM)
    q = (_dot(x.astype(BF16), wq_ref[...]) * scale).astype(BF16)
    outs = []
    for h in range(X_HEADS):
        sl = slice(h * X_HEAD_DIM, (h + 1) * X_HEAD_DIM)
        s = _dot_nt(q[:, sl], k_ref[:, sl])
        p = jnp.exp(s - jnp.max(s, axis=-1, keepdims=True))
        l = jnp.sum(p, axis=-1, keepdims=True)
        outs.append(_dot(p.astype(BF16), v_ref[:, sl]) / l)
    o = jnp.concatenate(outs, axis=1).astype(BF16)
    y = _dot(o, wo_ref[...])
    o_ref[...] = _layer_norm(alpha * x + y, lng_ref[...], lnb_ref[...])


def _xattn(x2d, k_mem, v_mem, w_q, w_o, ln_g, ln_b, alpha, T):
    n = x2d.shape[0]
    tm = ROW_TILE
    tpb = T // tm
    row = pl.BlockSpec((tm, D_MODEL), lambda i: (i, 0))
    mem = pl.BlockSpec((N_MEM, D_MODEL), lambda i: (i // tpb, 0))
    vec = lambda a: a.astype(F32)[None, :]
    consts = [w_q.astype(BF16), w_o.astype(BF16), vec(ln_g), vec(ln_b)]
    return pl.pallas_call(
        functools.partial(_xattn_kernel, alpha=alpha),
        grid=(n // tm,),
        in_specs=[row, mem, mem] + [_const_spec(a.shape) for a in consts],
        out_specs=row,
        out_shape=jax.ShapeDtypeStruct((n, D_MODEL), F32),
        compiler_params=_cparams(1),
        name="mem_xattn_ln2",
    )(x2d, k_mem, v_mem, *consts)


def _mlp_kernel(x_ref, wu_ref, wd_ref, lng_ref, lnb_ref, o_ref, *, alpha):
    x = x_ref[...]
    h = jnp.maximum(_dot(x.astype(BF16), wu_ref[...]), 0.0)
    y = _dot((h * h).astype(BF16), wd_ref[...])
    o_ref[...] = _layer_norm(alpha * x + y, lng_ref[...], lnb_ref[...])


def _mlp(x2d, w_up, w_down, ln_g, ln_b, alpha):
    n = x2d.shape[0]
    tm = ROW_TILE
    row = pl.BlockSpec((tm, D_MODEL), lambda i: (i, 0))
    vec = lambda a: a.astype(F32)[None, :]
    consts = [w_up.astype(BF16), w_down.astype(BF16), vec(ln_g), vec(ln_b)]
    return pl.pallas_call(
        functools.partial(_mlp_kernel, alpha=alpha),
        grid=(n // tm,),
        in_specs=[row] + [_const_spec(a.shape) for a in consts],
        out_specs=row,
        out_shape=jax.ShapeDtypeStruct((n, D_MODEL), F32),
        compiler_params=_cparams(1),
        name="mlp_ln3",
    )(x2d, *consts)


def _run_group(x, mem, p, depth):
    B, T, _ = x.shape
    alpha = (2 * depth) ** 0.25
    x2d = x.reshape(B * T, D_MODEL)
    mem2d = mem.reshape(B * N_MEM, D_MODEL)
    cos, sin = _rope_tables(T)
    for l in range(depth):
        qt, k, vt, qh, gfw, gbw, ih, gh, ga, gb = _inproj(
            x2d, p["w_in"][l].astype(BF16), cos, sin, p["q_norm"][l], p["k_norm"][l], p["hg_lb"], l, T)
        o_at = _attention(qt, k, vt, T)
        o_fw, o_bw = _hgrn(qh, gfw, gbw, ih, T)
        x2d = _merge(x2d, o_at, o_fw, o_bw, gh, ga, gb, p["w_pa"][l], p["w_pb"][l], p["w_out"][l],
                     p["hg_gnorm"][l], p["ln1_g"][l], p["ln1_b"][l], alpha)
        k_mem, v_mem = _memkv(mem2d, p["w_xk"][l], p["w_xv"][l])
        x2d = _xattn(x2d, k_mem, v_mem, p["w_xq"][l], p["w_xo"][l], p["ln2_g"][l], p["ln2_b"][l], alpha, T)
        x2d = _mlp(x2d, p["w_up"][l], p["w_down"][l], p["ln3_g"][l], p["ln3_b"][l], alpha)
    return x2d.reshape(B, T, D_MODEL)


def kernel(x_prompt, x_sample, mem_prompt, mem_sample, w_in, w_pa, w_pb, w_out, q_norm, k_norm, hg_lb, hg_gnorm, ln1_g, ln1_b, w_xq, w_xk, w_xv, w_xo, ln2_g, ln2_b, w_up, w_down, ln3_g, ln3_b):
    p = dict(w_in=w_in, w_pa=w_pa, w_pb=w_pb, w_out=w_out, q_norm=q_norm, k_norm=k_norm, hg_lb=hg_lb,
             hg_gnorm=hg_gnorm, ln1_g=ln1_g, ln1_b=ln1_b, w_xq=w_xq, w_xk=w_xk, w_xv=w_xv, w_xo=w_xo,
             ln2_g=ln2_g, ln2_b=ln2_b, w_up=w_up, w_down=w_down, ln3_g=ln3_g, ln3_b=ln3_b)
    depth = w_in.shape[0]
    return (_run_group(x_prompt, mem_prompt, p, depth), _run_group(x_sample, mem_sample, p, depth))
```

```python
import functools
import math

import numpy as np
import jax
import jax.numpy as jnp
from jax import lax
from jax.experimental import pallas as pl
from jax.experimental.pallas import tpu as pltpu

F32 = jnp.float32
BF16 = jnp.bfloat16

D_MODEL = 1024
GRID_W = 64
N_HEADS = 8
N_KV_HEADS = 2
HEAD_DIM = 64
ATTN_WIDTH = N_HEADS * HEAD_DIM
KV_WIDTH = N_KV_HEADS * HEAD_DIM
AXIS_DIM = HEAD_DIM // 2
ROPE_THETA = 10000.0
HG_HEADS = 4
HG_EXPAND = 128
HG_WIDTH = HG_HEADS * HG_EXPAND
N_MEM = 256
X_HEADS = 4
X_HEAD_DIM = D_MODEL // X_HEADS
D_FF = 4 * D_MODEL
EPS = 1e-6

LANES = 128
VMEM_LIMIT_V7X = 56 * 1024 * 1024

ROW_TILE = 512
ATTN_TQ = 256
ATTN_TK = 1024
HG_CHUNK = 128
NEG_BIG = -1e30


def _cparams(n_axes):
    return pltpu.CompilerParams(
        dimension_semantics=("arbitrary",) * n_axes,
        vmem_limit_bytes=VMEM_LIMIT_V7X,
    )


def _const_spec(shape):
    nd = len(shape)
    return pl.BlockSpec(shape, lambda *_: (0,) * nd, pipeline_mode=pl.Buffered(1))


def _dot(a, b):
    return jnp.dot(a, b, preferred_element_type=F32)


def _dot_nt(a, b):
    return lax.dot_general(a, b, (((1,), (1,)), ((), ())), preferred_element_type=F32)


def _dot_tn(a, b):
    return lax.dot_general(a, b, (((0,), (0,)), ((), ())), preferred_element_type=F32)


def _sigmoid(x):
    return 0.5 * jnp.tanh(0.5 * x) + 0.5


def _layer_norm(z, g, b):
    mu = jnp.mean(z, axis=-1, keepdims=True)
    zc = z - mu
    var = jnp.mean(zc * zc, axis=-1, keepdims=True)
    return zc * lax.rsqrt(var + EPS) * g + b


_A_Q0 = 0
_A_K0 = _A_Q0 + ATTN_WIDTH
_A_V0 = _A_K0 + KV_WIDTH
_A_HG0 = _A_V0 + KV_WIDTH
_A_GA0 = _A_HG0 + 5 * HG_WIDTH
_A_GB0 = _A_GA0 + D_MODEL
_NORM_CHUNK = 256
LOG2E = math.log2(math.e)


def _head_rmsnorm_rope(h, gain, cos, sin, bd, swap_lo):
    sq = h * h
    hi = sq.astype(BF16)
    lo = (sq - hi.astype(F32)).astype(BF16)
    ms = _dot(hi, bd) + _dot(lo, bd)
    hn = h * lax.rsqrt(ms + EPS) * gain
    n = h.shape[1]
    half = AXIS_DIM // 2
    partner = jnp.where(swap_lo, pltpu.roll(hn, n - half, 1), pltpu.roll(hn, half, 1))
    return hn * cos + partner * sin


def _inproj_kernel(x_ref, w_ref, cos_ref, sin_ref, qg_ref, kg_ref, bd_ref, lb_ref,
                   qt_out, k_out, vt_out, qh_out, gfw_out, gbw_out, ih_out, gh_out, ga_out, gb_out, *, layer):
    xb = x_ref[...].astype(BF16)
    cos = cos_ref[...]
    sin = sin_ref[...]
    bd = bd_ref[...]

    def swap_lo(n):
        lane = lax.broadcasted_iota(jnp.int32, (xb.shape[0], n), 1)
        return (lane & (AXIS_DIM - 1)) < (AXIS_DIM // 2)

    def proj(c0, n):
        return _dot(xb, w_ref[:, c0:c0 + n])

    scale = LOG2E / math.sqrt(HEAD_DIM)
    for c in range(ATTN_WIDTH // _NORM_CHUNK):
        h = proj(_A_Q0 + c * _NORM_CHUNK, _NORM_CHUNK)
        r = _head_rmsnorm_rope(h, qg_ref[...], cos, sin, bd, swap_lo(_NORM_CHUNK))
        qt_out[c * _NORM_CHUNK:(c + 1) * _NORM_CHUNK, :] = (r * scale).T.astype(qt_out.dtype)
    kw = KV_WIDTH
    kv = proj(_A_K0, 2 * kw)
    k_out[...] = _head_rmsnorm_rope(kv[:, :kw], kg_ref[...], cos[:, :kw], sin[:, :kw], bd[:kw, :kw],
                                    swap_lo(kw)).astype(k_out.dtype)
    vt_out[...] = kv[:, kw:].T.astype(vt_out.dtype)

    def hg(i):
        return proj(_A_HG0 + i * HG_WIDTH, HG_WIDTH)

    raw = lb_ref[...]
    e = jnp.exp(raw - jnp.max(raw, axis=1, keepdims=True))
    lbs = jnp.sum(e[:, :layer + 1, :], axis=1) / jnp.sum(e, axis=1)
    qh = hg(0)
    qh_out[...] = qh * _sigmoid(qh)
    for d, o in enumerate((gfw_out, gbw_out)):
        lb = lbs[d:d + 1, :]
        o[...] = jnp.log2(lb + (1.0 - lb) * _sigmoid(hg(1 + d)))
    ih_out[...] = hg(3).astype(ih_out.dtype)
    gh = hg(4)
    gh_out[...] = (gh * _sigmoid(gh)).astype(gh_out.dtype)
    ga_out[...] = _sigmoid(proj(_A_GA0, D_MODEL)).astype(ga_out.dtype)
    gb_out[...] = _sigmoid(proj(_A_GB0, D_MODEL)).astype(gb_out.dtype)


def _rope_tables(T):
    rows = T // GRID_W
    row = jnp.repeat(jnp.arange(rows, dtype=F32), GRID_W)
    col = jnp.tile(jnp.arange(GRID_W, dtype=F32), rows)
    inv_freq = ROPE_THETA ** (-jnp.arange(0, AXIS_DIM, 2, dtype=F32) / AXIS_DIM)
    ar = row[:, None] * inv_freq
    ac = col[:, None] * inv_freq
    cos = jnp.concatenate([jnp.cos(ar), jnp.cos(ar), jnp.cos(ac), jnp.cos(ac)], axis=1)
    sin = jnp.concatenate([-jnp.sin(ar), jnp.sin(ar), -jnp.sin(ac), jnp.sin(ac)], axis=1)
    reps = _NORM_CHUNK // HEAD_DIM
    return jnp.tile(cos, (1, reps)), jnp.tile(sin, (1, reps))


def _inproj(x2d, w_b, cos, sin, q_norm, k_norm, hg_lb, layer, T):
    n = x2d.shape[0]
    lb = hg_lb.astype(F32)
    tm = ROW_TILE
    tpb = T // tm
    reps = _NORM_CHUNK // HEAD_DIM
    qg = jnp.tile(q_norm.astype(F32), reps)[None, :]
    kg = jnp.tile(k_norm.astype(F32), KV_WIDTH // HEAD_DIM)[None, :]
    bd = jnp.asarray(np.kron(np.eye(reps, dtype=np.float32),
                             np.full((HEAD_DIM, HEAD_DIM), 1.0 / HEAD_DIM, np.float32)), BF16)
    row = lambda w: pl.BlockSpec((tm, w), lambda i: (i, 0))
    col = lambda h: pl.BlockSpec((h, tm), lambda i: (0, i))
    tab = pl.BlockSpec((tm, _NORM_CHUNK), lambda i: (i % tpb, 0))
    out = lambda w, dt: jax.ShapeDtypeStruct((n, w), dt)
    return pl.pallas_call(
        functools.partial(_inproj_kernel, layer=layer),
        grid=(n // tm,),
        in_specs=[row(D_MODEL), _const_spec(w_b.shape), tab, tab,
                  _const_spec(qg.shape), _const_spec(kg.shape), _const_spec(bd.shape), _const_spec(lb.shape)],
        out_specs=[col(ATTN_WIDTH), row(KV_WIDTH), col(KV_WIDTH)] + [row(HG_WIDTH)] * 5 + [row(D_MODEL)] * 2,
        out_shape=[jax.ShapeDtypeStruct((ATTN_WIDTH, n), BF16), jax.ShapeDtypeStruct((n, KV_WIDTH), BF16),
                   jax.ShapeDtypeStruct((KV_WIDTH, n), BF16)]
                  + [out(HG_WIDTH, dt) for dt in (F32, F32, F32, BF16, BF16)] + [out(D_MODEL, BF16)] * 2,
        compiler_params=_cparams(1),
        name="inproj",
    )(x2d, w_b, cos, sin, qg, kg, bd, lb)


_ATTN_HEADS_PER_LOOP = 2
_ATTN_STRAIGHT_LINE_BLOCKS = 4
_ONES_ROWS = 16


def _attn_kernel(qt_ref, k_ref, vt_ref, ot_ref, sa_ref, sb_ref, *, tk):
    tq = qt_ref.shape[1]
    nkb = k_ref.shape[0] // tk
    unroll = nkb if nkb <= _ATTN_STRAIGHT_LINE_BLOCKS else 2
    group = N_HEADS // N_KV_HEADS
    ones = jnp.ones((_ONES_ROWS, tk), BF16)
    zeros = jnp.zeros((HEAD_DIM, tq), BF16)
    for h0 in range(0, N_HEADS, _ATTN_HEADS_PER_LOOP):
        heads = range(h0, h0 + _ATTN_HEADS_PER_LOOP)
        kv = h0 // group
        ws = []
        for h in heads:
            qh = qt_ref[h * HEAD_DIM:(h + 1) * HEAD_DIM, :]
            parts = [zeros] * N_KV_HEADS
            parts[kv] = qh
            ws.append(jnp.concatenate(parts, axis=0))

        def scores(kb, s_ref, ws=ws):
            kblk = k_ref[pl.ds(pl.multiple_of(kb * tk, tk), tk), :]
            for i, w in enumerate(ws):
                s_ref[i] = _dot(kblk, w)

        def process(kb, s_ref, carry, kv=kv):
            start = pl.multiple_of(kb * tk, tk)
            vext = jnp.concatenate([vt_ref[kv * HEAD_DIM:(kv + 1) * HEAD_DIM, pl.ds(start, tk)], ones], axis=0)
            out = []
            for i, (m, acc) in enumerate(carry):
                st = s_ref[i]
                m_new = jnp.maximum(m, jnp.max(st, axis=0, keepdims=True))
                alpha = jnp.exp2(m - m_new)
                pt = jnp.exp2(st - m_new).astype(BF16)
                out.append((m_new, acc * alpha + _dot(vext, pt)))
            return tuple(out)

        bufs = (sa_ref, sb_ref)

        def run_blocks(first, carry, last):
            for u in range(unroll):
                if not (last and u == unroll - 1):
                    scores(first + u + 1, bufs[(u + 1) % 2])
                carry = process(first + u, bufs[u % 2], carry)
            return carry

        init = tuple((jnp.full((1, tq), NEG_BIG, F32), jnp.zeros((HEAD_DIM + _ONES_ROWS, tq), F32))
                     for _ in heads)
        scores(0, sa_ref)
        carry = lax.fori_loop(0, nkb // unroll - 1, lambda i, c: run_blocks(i * unroll, c, False), init)
        carry = run_blocks(nkb - unroll, carry, True)
        for h, (_, acc) in zip(heads, carry):
            ot_ref[h * HEAD_DIM:(h + 1) * HEAD_DIM, :] = (
                acc[:HEAD_DIM] / acc[HEAD_DIM:HEAD_DIM + 1]).astype(ot_ref.dtype)


def _attention(qt, k, vt, T):
    n = k.shape[0]
    tq = min(ATTN_TQ, T)
    tk = min(ATTN_TK, T // 4)
    nq = T // tq
    assert (T // tk) % 2 == 0, "the two score buffers alternate"
    q_spec = pl.BlockSpec((ATTN_WIDTH, tq), lambda b, i: (0, b * nq + i))
    s_buf = pltpu.VMEM((_ATTN_HEADS_PER_LOOP, tk, tq), F32)
    return pl.pallas_call(
        functools.partial(_attn_kernel, tk=tk),
        grid=(n // T, nq),
        in_specs=[q_spec, pl.BlockSpec((T, KV_WIDTH), lambda b, i: (b, 0)),
                  pl.BlockSpec((KV_WIDTH, T), lambda b, i: (0, b))],
        out_specs=q_spec,
        out_shape=jax.ShapeDtypeStruct((ATTN_WIDTH, n), BF16),
        scratch_shapes=[s_buf, s_buf],
        compiler_params=_cparams(2),
        name="gqa_attention",
    )(qt, k, vt)


SUBLANES = 8


def _pivot_bcast(p, row, m, pivot):
    c, dk = p.shape
    blk = 2 * m
    p3 = p.reshape(c // SUBLANES, SUBLANES, dk)
    if blk == SUBLANES:
        return jnp.broadcast_to(p3[:, pivot:pivot + 1, :], p3.shape).reshape(c, dk)
    row3 = row.reshape(p3.shape)
    for j in range(int(math.log2(blk))):
        step = 1 << j
        if (pivot >> j) & 1:
            p3 = jnp.where((row3 & step) == 0, pltpu.roll(p3, SUBLANES - step, 1), p3)
        else:
            p3 = jnp.where((row3 & step) != 0, pltpu.roll(p3, step, 1), p3)
    return p3.reshape(c, dk)


_HG_HEADS_INTERLEAVED = 4
_DONE = object()


def _hgrn_direction(q, g, v, lvl_ref, st_ref, o_ref, reverse):
    c = q.shape[0]
    f = jnp.exp2(g)
    k = 1.0 - f
    row = lax.broadcasted_iota(jnp.int32, q.shape, 0)
    p = g
    a = jnp.zeros((c, c), F32)
    for level in range(int(math.log2(c))):
        m = 1 << level
        if m < SUBLANES:
            qside = ((row & m) == 0) if reverse else ((row & m) != 0)
            r = _pivot_bcast(p, row, m, m if reverse else m - 1)
            if level == 0:
                y = jnp.where(qside, q * f, k)
            else:
                y = jnp.exp2(jnp.where(qside, p, r - p)) * jnp.where(qside, q, k)
            p = p + jnp.where(qside, r, 0.0)
        else:
            ys, ps = [], []
            for b in range(0, c, 2 * m):
                lo, hi = slice(b, b + m), slice(b + m, b + 2 * m)
                src, qry = (hi, lo) if reverse else (lo, hi)
                piv = b + m if reverse else b + m - 1
                r = p[piv:piv + 1, :]
                y_src = jnp.exp2(r - p[src]) * k[src]
                y_qry = jnp.exp2(p[qry]) * q[qry]
                p_qry = p[qry] + r
                ys += [y_qry, y_src] if reverse else [y_src, y_qry]
                ps += [p_qry, p[src]] if reverse else [p[src], p_qry]
            y = jnp.concatenate(ys, axis=0)
            p = jnp.concatenate(ps, axis=0)
        yb = y.astype(BF16)
        scores = _dot_nt(yb, yb)
        yield
        a = jnp.where(lvl_ref[...] == level, scores, a)
    diag = jnp.sum(q * k, axis=1, keepdims=True)
    st = st_ref[...]
    intra = _dot(a.astype(BF16), v)
    inter = _dot_nt((q * jnp.exp2(p)).astype(BF16), st.astype(BF16))
    tot = p[0:1, :] if reverse else p[c - 1:c, :]
    kd = k * jnp.exp2(tot - p)
    update = _dot_tn(v, kd.astype(BF16))
    yield
    o_ref[...] = intra + diag * v.astype(F32) + inter
    st_ref[...] = jnp.exp2(tot) * st + update


def _hgrn_kernel(qf_ref, gf_ref, vf_ref, qb_ref, gb_ref, vb_ref, lvlf_ref, lvlb_ref,
                 of_ref, ob_ref, st_ref):
    @pl.when(pl.program_id(1) == 0)
    def _():
        st_ref[...] = jnp.zeros_like(st_ref)

    for h0 in range(0, HG_HEADS, _HG_HEADS_INTERLEAVED):
        chunks = []
        for h in range(h0, h0 + _HG_HEADS_INTERLEAVED):
            sl = slice(h * HG_EXPAND, (h + 1) * HG_EXPAND)
            chunks.append(_hgrn_direction(qf_ref[:, sl], gf_ref[:, sl], vf_ref[:, sl], lvlf_ref,
                                          st_ref.at[0, h], of_ref.at[:, sl], False))
            chunks.append(_hgrn_direction(qb_ref[:, sl], gb_ref[:, sl], vb_ref[:, sl], lvlb_ref,
                                          st_ref.at[1, h], ob_ref.at[:, sl], True))
        while chunks:
            chunks = [ch for ch in chunks if next(ch, _DONE) is not _DONE]


def _level_tables(c):
    t = np.arange(c)[:, None]
    s = np.arange(c)[None, :]
    x = t ^ s
    lv = np.where(x > 0, np.floor(np.log2(np.maximum(x, 1))), -1).astype(np.int32)
    fw = np.where(t > s, lv, -1).astype(np.int32)
    bw = np.where(t < s, lv, -1).astype(np.int32)
    return jnp.asarray(fw), jnp.asarray(bw)


def _hgrn(qh, gfw, gbw, ih, T):
    n = qh.shape[0]
    c = min(HG_CHUNK, T)
    nc = T // c
    lvl_fw, lvl_bw = _level_tables(c)
    fw = pl.BlockSpec((c, HG_WIDTH), lambda b, i: (b * nc + i, 0))
    bw = pl.BlockSpec((c, HG_WIDTH), lambda b, i: (b * nc + nc - 1 - i, 0))
    out = jax.ShapeDtypeStruct((n, HG_WIDTH), F32)
    return pl.pallas_call(
        _hgrn_kernel,
        grid=(n // T, nc),
        in_specs=[fw, fw, fw, bw, bw, bw, _const_spec((c, c)), _const_spec((c, c))],
        out_specs=[fw, bw],
        out_shape=[out, out],
        scratch_shapes=[pltpu.VMEM((2, HG_HEADS, HG_EXPAND, HG_EXPAND), F32)],
        compiler_params=_cparams(2),
        name="hgrn2_bidir",
    )(qh, gfw, ih, qh, gbw, ih, lvl_fw, lvl_bw)


def _merge_kernel(x_ref, oat_ref, of_ref, ob_ref, gh_ref, ga_ref, gb_ref,
                  wpa_ref, wpb_ref, wout_ref, gn_ref, lng_ref, lnb_ref, o_ref, *, alpha):
    o = of_ref[...] + ob_ref[...]
    gn = gn_ref[...]
    parts = []
    for h in range(HG_HEADS):
        oh = o[:, h * HG_EXPAND:(h + 1) * HG_EXPAND]
        ms = jnp.mean(oh * oh, axis=-1, keepdims=True)
        parts.append(oh * lax.rsqrt(ms + EPS) * gn)
    on = jnp.concatenate(parts, axis=1)
    o_b = (on * gh_ref[...].astype(F32)).astype(BF16)
    pa = _dot_tn(oat_ref[...], wpa_ref[...])
    pb = _dot(o_b, wpb_ref[...])
    merged = ga_ref[...].astype(F32) * pa + gb_ref[...].astype(F32) * pb
    y = _dot(merged.astype(BF16), wout_ref[...])
    o_ref[...] = _layer_norm(alpha * x_ref[...] + y, lng_ref[...], lnb_ref[...])


def _merge(x2d, o_at, o_fw, o_bw, gh, ga, gb, w_pa, w_pb, w_out, g_norm, ln_g, ln_b, alpha):
    n = x2d.shape[0]
    tm = ROW_TILE
    row = lambda w: pl.BlockSpec((tm, w), lambda i: (i, 0))
    vec = lambda a: a.astype(F32)[None, :]
    consts = [w_pa.astype(BF16), w_pb.astype(BF16), w_out.astype(BF16), vec(g_norm), vec(ln_g), vec(ln_b)]
    return pl.pallas_call(
        functools.partial(_merge_kernel, alpha=alpha),
        grid=(n // tm,),
        in_specs=[row(D_MODEL), pl.BlockSpec((ATTN_WIDTH, tm), lambda i: (0, i)),
                  row(HG_WIDTH), row(HG_WIDTH), row(HG_WIDTH),
                  row(D_MODEL), row(D_MODEL)] + [_const_spec(a.shape) for a in consts],
        out_specs=row(D_MODEL),
        out_shape=jax.ShapeDtypeStruct((n, D_MODEL), F32),
        compiler_params=_cparams(1),
        name="merge_ln1",
    )(x2d, o_at, o_fw, o_bw, gh, ga, gb, *consts)


def _memkv_kernel(m_ref, wk_ref, wv_ref, k_out, v_out):
    mb = m_ref[...].astype(BF16)
    k_out[...] = _dot(mb, wk_ref[...]).astype(k_out.dtype)
    v_out[...] = _dot(mb, wv_ref[...]).astype(v_out.dtype)


def _memkv(mem2d, w_k, w_v):
    n = mem2d.shape[0]
    tm = N_MEM
    row = pl.BlockSpec((tm, D_MODEL), lambda i: (i, 0))
    out = jax.ShapeDtypeStruct((n, D_MODEL), BF16)
    return pl.pallas_call(
        _memkv_kernel,
        grid=(n // tm,),
        in_specs=[row, _const_spec(w_k.shape), _const_spec(w_v.shape)],
        out_specs=[row, row],
        out_shape=[out, out],
        compiler_params=_cparams(1),
        name="mem_kv",
    )(mem2d, w_k.astype(BF16), w_v.astype(BF16))


def _xattn_kernel(x_ref, k_ref, v_ref, wq_ref, wo_ref, lng_ref, lnb_ref, o_ref, *, alpha):
    x = x_ref[...]
    scale = 1.0 / math.sqrt(X_HEAD_DIM)
    q = (_dot(x.astype(BF16), wq_ref[...]) * scale).astype(BF16)
    outs = []
    for h in range(X_HEADS):
        sl = slice(h * X_HEAD_DIM, (h + 1) * X_HEAD_DIM)
        s = _dot_nt(q[:, sl], k_ref[:, sl])
        p = jnp.exp(s - jnp.max(s, axis=-1, keepdims=True))
        l = jnp.sum(p, axis=-1, keepdims=True)
        outs.append(_dot(p.astype(BF16), v_ref[:, sl]) / l)
    o = jnp.concatenate(outs, axis=1).astype(BF16)
    y = _dot(o, wo_ref[...])
    o_ref[...] = _layer_norm(alpha * x + y, lng_ref[...], lnb_ref[...])


def _xattn(x2d, k_mem, v_mem, w_q, w_o, ln_g, ln_b, alpha, T):
    n = x2d.shape[0]
    tm = ROW_TILE
    tpb = T // tm
    row = pl.BlockSpec((tm, D_MODEL), lambda i: (i, 0))
    mem = pl.BlockSpec((N_MEM, D_MODEL), lambda i: (i // tpb, 0))
    vec = lambda a: a.astype(F32)[None, :]
    consts = [w_q.astype(BF16), w_o.astype(BF16), vec(ln_g), vec(ln_b)]
    return pl.pallas_call(
        functools.partial(_xattn_kernel, alpha=alpha),
        grid=(n // tm,),
        in_specs=[row, mem, mem] + [_const_spec(a.shape) for a in consts],
        out_specs=row,
        out_shape=jax.ShapeDtypeStruct((n, D_MODEL), F32),
        compiler_params=_cparams(1),
        name="mem_xattn_ln2",
    )(x2d, k_mem, v_mem, *consts)


def _mlp_kernel(x_ref, wu_ref, wd_ref, lng_ref, lnb_ref, o_ref, *, alpha):
    x = x_ref[...]
    h = jnp.maximum(_dot(x.astype(BF16), wu_ref[...]), 0.0)
    y = _dot((h * h).astype(BF16), wd_ref[...])
    o_ref[...] = _layer_norm(alpha * x + y, lng_ref[...], lnb_ref[...])


def _mlp(x2d, w_up, w_down, ln_g, ln_b, alpha):
    n = x2d.shape[0]
    tm = ROW_TILE
    row = pl.BlockSpec((tm, D_MODEL), lambda i: (i, 0))
    vec = lambda a: a.astype(F32)[None, :]
    consts = [w_up.astype(BF16), w_down.astype(BF16), vec(ln_g), vec(ln_b)]
    return pl.pallas_call(
        functools.partial(_mlp_kernel, alpha=alpha),
        grid=(n // tm,),
        in_specs=[row] + [_const_spec(a.shape) for a in consts],
        out_specs=row,
        out_shape=jax.ShapeDtypeStruct((n, D_MODEL), F32),
        compiler_params=_cparams(1),
        name="mlp_ln3",
    )(x2d, *consts)


def _run_group(x, mem, p, depth):
    B, T, _ = x.shape
    alpha = (2 * depth) ** 0.25
    x2d = x.reshape(B * T, D_MODEL)
    mem2d = mem.reshape(B * N_MEM, D_MODEL)
    cos, sin = _rope_tables(T)
    for l in range(depth):
        qt, k, vt, qh, gfw, gbw, ih, gh, ga, gb = _inproj(
            x2d, p["w_in"][l].astype(BF16), cos, sin, p["q_norm"][l], p["k_norm"][l], p["hg_lb"], l, T)
        o_at = _attention(qt, k, vt, T)
        o_fw, o_bw = _hgrn(qh, gfw, gbw, ih, T)
        x2d = _merge(x2d, o_at, o_fw, o_bw, gh, ga, gb, p["w_pa"][l], p["w_pb"][l], p["w_out"][l],
                     p["hg_gnorm"][l], p["ln1_g"][l], p["ln1_b"][l], alpha)
        k_mem, v_mem = _memkv(mem2d, p["w_xk"][l], p["w_xv"][l])
        x2d = _xattn(x2d, k_mem, v_mem, p["w_xq"][l], p["w_xo"][l], p["ln2_g"][l], p["ln2_b"][l], alpha, T)
        x2d = _mlp(x2d, p["w_up"][l], p["w_down"][l], p["ln3_g"][l], p["ln3_b"][l], alpha)
    return x2d.reshape(B, T, D_MODEL)


def kernel(x_prompt, x_sample, mem_prompt, mem_sample, w_in, w_pa, w_pb, w_out, q_norm, k_norm, hg_lb, hg_gnorm, ln1_g, ln1_b, w_xq, w_xk, w_xv, w_xo, ln2_g, ln2_b, w_up, w_down, ln3_g, ln3_b):
    p = dict(w_in=w_in, w_pa=w_pa, w_pb=w_pb, w_out=w_out, q_norm=q_norm, k_norm=k_norm, hg_lb=hg_lb,
             hg_gnorm=hg_gnorm, ln1_g=ln1_g, ln1_b=ln1_b, w_xq=w_xq, w_xk=w_xk, w_xv=w_xv, w_xo=w_xo,
             ln2_g=ln2_g, ln2_b=ln2_b, w_up=w_up, w_down=w_down, ln3_g=ln3_g, ln3_b=ln3_b)
    depth = w_in.shape[0]
    return (_run_group(x_prompt, mem_prompt, p, depth), _run_group(x_sample, mem_sample, p, depth))
```

```python
import functools
import math

import numpy as np
import jax
import jax.numpy as jnp
from jax import lax
from jax.experimental import pallas as pl
from jax.experimental.pallas import tpu as pltpu

F32 = jnp.float32
BF16 = jnp.bfloat16

D_MODEL = 1024
GRID_W = 64
N_HEADS = 8
N_KV_HEADS = 2
HEAD_DIM = 64
ATTN_WIDTH = N_HEADS * HEAD_DIM
KV_WIDTH = N_KV_HEADS * HEAD_DIM
AXIS_DIM = HEAD_DIM // 2
ROPE_THETA = 10000.0
HG_HEADS = 4
HG_EXPAND = 128
HG_WIDTH = HG_HEADS * HG_EXPAND
N_MEM = 256
X_HEADS = 4
X_HEAD_DIM = D_MODEL // X_HEADS
D_FF = 4 * D_MODEL
EPS = 1e-6

LANES = 128
SUBLANES = 8
VMEM_LIMIT_V7X = 56 * 1024 * 1024

ROW_TILE = 512
ATTN_TQ = 256
ATTN_TK = 1024
HG_CHUNK = 128
HG_SEQS = 4
NEG_BIG = -1e30


def _cparams(n_axes):
    return pltpu.CompilerParams(
        dimension_semantics=("arbitrary",) * n_axes,
        vmem_limit_bytes=VMEM_LIMIT_V7X,
    )


def _const_spec(shape):
    nd = len(shape)
    return pl.BlockSpec(shape, lambda *_: (0,) * nd, pipeline_mode=pl.Buffered(1))


def _dot(a, b):
    return jnp.dot(a, b, preferred_element_type=F32)


def _dot_nt(a, b):
    return lax.dot_general(a, b, (((1,), (1,)), ((), ())), preferred_element_type=F32)


def _dot_tn(a, b):
    return lax.dot_general(a, b, (((0,), (0,)), ((), ())), preferred_element_type=F32)


def _sigmoid(x):
    return 0.5 * jnp.tanh(0.5 * x) + 0.5


def _layer_norm(z, g, b):
    mu = jnp.mean(z, axis=-1, keepdims=True)
    zc = z - mu
    var = jnp.mean(zc * zc, axis=-1, keepdims=True)
    return zc * lax.rsqrt(var + EPS) * g + b


_A_Q0 = 0
_A_K0 = _A_Q0 + ATTN_WIDTH
_A_HG0 = _A_K0 + 2 * KV_WIDTH
_A_GA0 = _A_HG0 + 5 * HG_WIDTH
_A_GB0 = _A_GA0 + D_MODEL
_NORM_CHUNK = 256
LOG2E = math.log2(math.e)


def _head_rmsnorm_rope(h, gain, cos, sin, bd, swap_lo):
    sq = h * h
    hi = sq.astype(BF16)
    lo = (sq - hi.astype(F32)).astype(BF16)
    ms = _dot(hi, bd) + _dot(lo, bd)
    hn = h * lax.rsqrt(ms + EPS) * gain
    n = h.shape[1]
    half = AXIS_DIM // 2
    partner = jnp.where(swap_lo, pltpu.roll(hn, n - half, 1), pltpu.roll(hn, half, 1))
    return hn * cos + partner * sin


def _inproj_kernel(x_ref, w_ref, cos_ref, sin_ref, qg_ref, kg_ref, bd_ref, lb_ref,
                   qt_out, k_out, vt_out, qh_out, gfw_out, gbw_out, ih_out, gh_out, ga_out, gb_out, *, layer):
    xb = x_ref[...].astype(BF16)
    cos = cos_ref[...]
    sin = sin_ref[...]
    bd = bd_ref[...]

    def swap_lo(n):
        lane = lax.broadcasted_iota(jnp.int32, (xb.shape[0], n), 1)
        return (lane & (AXIS_DIM - 1)) < (AXIS_DIM // 2)

    def proj(c0, n):
        return _dot(xb, w_ref[:, c0:c0 + n])

    scale = LOG2E / math.sqrt(HEAD_DIM)
    for c in range(ATTN_WIDTH // _NORM_CHUNK):
        h = proj(_A_Q0 + c * _NORM_CHUNK, _NORM_CHUNK)
        r = _head_rmsnorm_rope(h, qg_ref[...], cos, sin, bd, swap_lo(_NORM_CHUNK))
        qt_out[c * _NORM_CHUNK:(c + 1) * _NORM_CHUNK, :] = (r * scale).T.astype(qt_out.dtype)
    kw = KV_WIDTH
    kv = proj(_A_K0, 2 * kw)
    k_out[...] = _head_rmsnorm_rope(kv[:, :kw], kg_ref[...], cos[:, :kw], sin[:, :kw], bd[:kw, :kw],
                                    swap_lo(kw)).astype(k_out.dtype)
    vt_out[...] = kv[:, kw:].T.astype(vt_out.dtype)

    def hg(i):
        return proj(_A_HG0 + i * HG_WIDTH, HG_WIDTH)

    raw = lb_ref[...]
    e = jnp.exp(raw - jnp.max(raw, axis=1, keepdims=True))
    lbs = jnp.sum(e[:, :layer + 1, :], axis=1) / jnp.sum(e, axis=1)
    qh = hg(0)
    qh_out[...] = qh * _sigmoid(qh)
    for d, o in enumerate((gfw_out, gbw_out)):
        lb = lbs[d:d + 1, :]
        o[...] = jnp.log2(lb + (1.0 - lb) * _sigmoid(hg(1 + d)))
    ih_out[...] = hg(3).astype(ih_out.dtype)
    gh = hg(4)
    gh_out[...] = (gh * _sigmoid(gh)).astype(gh_out.dtype)
    ga_out[...] = _sigmoid(proj(_A_GA0, D_MODEL)).astype(ga_out.dtype)
    gb_out[...] = _sigmoid(proj(_A_GB0, D_MODEL)).astype(gb_out.dtype)


def _rope_tables(T):
    rows = T // GRID_W
    row = jnp.repeat(jnp.arange(rows, dtype=F32), GRID_W)
    col = jnp.tile(jnp.arange(GRID_W, dtype=F32), rows)
    inv_freq = ROPE_THETA ** (-jnp.arange(0, AXIS_DIM, 2, dtype=F32) / AXIS_DIM)
    ar = row[:, None] * inv_freq
    ac = col[:, None] * inv_freq
    cos = jnp.concatenate([jnp.cos(ar), jnp.cos(ar), jnp.cos(ac), jnp.cos(ac)], axis=1)
    sin = jnp.concatenate([-jnp.sin(ar), jnp.sin(ar), -jnp.sin(ac), jnp.sin(ac)], axis=1)
    reps = _NORM_CHUNK // HEAD_DIM
    return jnp.tile(cos, (1, reps)), jnp.tile(sin, (1, reps))


def _inproj(x2d, w_b, cos, sin, q_norm, k_norm, hg_lb, layer, T):
    n = x2d.shape[0]
    lb = hg_lb.astype(F32)
    tm = ROW_TILE
    tpb = T // tm
    reps = _NORM_CHUNK // HEAD_DIM
    qg = jnp.tile(q_norm.astype(F32), reps)[None, :]
    kg = jnp.tile(k_norm.astype(F32), KV_WIDTH // HEAD_DIM)[None, :]
    bd = jnp.asarray(np.kron(np.eye(reps, dtype=np.float32),
                             np.full((HEAD_DIM, HEAD_DIM), 1.0 / HEAD_DIM, np.float32)), BF16)
    row = lambda w: pl.BlockSpec((tm, w), lambda i: (i, 0))
    col = lambda h: pl.BlockSpec((h, tm), lambda i: (0, i))
    tab = pl.BlockSpec((tm, _NORM_CHUNK), lambda i: (i % tpb, 0))
    out = lambda w, dt: jax.ShapeDtypeStruct((n, w), dt)
    return pl.pallas_call(
        functools.partial(_inproj_kernel, layer=layer),
        grid=(n // tm,),
        in_specs=[row(D_MODEL), _const_spec(w_b.shape), tab, tab,
                  _const_spec(qg.shape), _const_spec(kg.shape), _const_spec(bd.shape), _const_spec(lb.shape)],
        out_specs=[col(ATTN_WIDTH), row(KV_WIDTH), col(KV_WIDTH)] + [row(HG_WIDTH)] * 5 + [row(D_MODEL)] * 2,
        out_shape=[jax.ShapeDtypeStruct((ATTN_WIDTH, n), BF16), jax.ShapeDtypeStruct((n, KV_WIDTH), BF16),
                   jax.ShapeDtypeStruct((KV_WIDTH, n), BF16)]
                  + [out(HG_WIDTH, dt) for dt in (F32, F32, F32, BF16, BF16)] + [out(D_MODEL, BF16)] * 2,
        compiler_params=_cparams(1),
        name="inproj",
    )(x2d, w_b, cos, sin, qg, kg, bd, lb)


_ATTN_STRAIGHT_LINE_BLOCKS = 4
_ONES_ROWS = 16


def _attn_kernel(qt_ref, k_ref, vt_ref, ot_ref, sa_ref, sb_ref, *, tk):
    tq = qt_ref.shape[1]
    nkb = k_ref.shape[0] // tk
    unroll = nkb if nkb <= _ATTN_STRAIGHT_LINE_BLOCKS else 2
    group = N_HEADS // N_KV_HEADS
    ones = jnp.ones((_ONES_ROWS, tk), BF16)
    zeros = jnp.zeros((HEAD_DIM, tq), BF16)
    together = sa_ref.shape[0]
    for h0 in range(0, N_HEADS, together):
        heads = range(h0, h0 + together)
        kv = h0 // group
        ws = []
        for h in heads:
            qh = qt_ref[h * HEAD_DIM:(h + 1) * HEAD_DIM, :]
            parts = [zeros] * N_KV_HEADS
            parts[kv] = qh
            ws.append(jnp.concatenate(parts, axis=0))

        def scores(kb, s_ref, ws=ws):
            kblk = k_ref[pl.ds(pl.multiple_of(kb * tk, tk), tk), :]
            for i, w in enumerate(ws):
                s_ref[i] = _dot(kblk, w)

        def process(kb, s_ref, carry, kv=kv):
            start = pl.multiple_of(kb * tk, tk)
            vext = jnp.concatenate([vt_ref[kv * HEAD_DIM:(kv + 1) * HEAD_DIM, pl.ds(start, tk)], ones], axis=0)
            out = []
            for i, (m, acc) in enumerate(carry):
                st = s_ref[i]
                m_new = jnp.maximum(m, jnp.max(st, axis=0, keepdims=True))
                alpha = jnp.exp2(m - m_new)
                pt = jnp.exp2(st - m_new).astype(BF16)
                out.append((m_new, acc * alpha + _dot(vext, pt)))
            return tuple(out)

        bufs = (sa_ref, sb_ref)

        def run_blocks(first, carry, last):
            for u in range(unroll):
                if not (last and u == unroll - 1):
                    scores(first + u + 1, bufs[(u + 1) % 2])
                carry = process(first + u, bufs[u % 2], carry)
            return carry

        init = tuple((jnp.full((1, tq), NEG_BIG, F32), jnp.zeros((HEAD_DIM + _ONES_ROWS, tq), F32))
                     for _ in heads)
        scores(0, sa_ref)
        carry = lax.fori_loop(0, nkb // unroll - 1, lambda i, c: run_blocks(i * unroll, c, False), init)
        carry = run_blocks(nkb - unroll, carry, True)
        for h, (_, acc) in zip(heads, carry):
            ot_ref[h * HEAD_DIM:(h + 1) * HEAD_DIM, :] = (
                acc[:HEAD_DIM] / acc[HEAD_DIM:HEAD_DIM + 1]).astype(ot_ref.dtype)


def _attention(qt, k, vt, T):
    n = k.shape[0]
    tq = min(ATTN_TQ, T)
    tk = min(ATTN_TK, T // 4)
    nq = T // tq
    assert (T // tk) % 2 == 0, "the two score buffers alternate"
    q_spec = pl.BlockSpec((ATTN_WIDTH, tq), lambda b, i: (0, b * nq + i))
    group = N_HEADS // N_KV_HEADS
    together = group if T // tk <= _ATTN_STRAIGHT_LINE_BLOCKS else group // 2
    s_buf = pltpu.VMEM((together, tk, tq), F32)
    return pl.pallas_call(
        functools.partial(_attn_kernel, tk=tk),
        grid=(n // T, nq),
        in_specs=[q_spec, pl.BlockSpec((T, KV_WIDTH), lambda b, i: (b, 0)),
                  pl.BlockSpec((KV_WIDTH, T), lambda b, i: (0, b))],
        out_specs=q_spec,
        out_shape=jax.ShapeDtypeStruct((ATTN_WIDTH, n), BF16),
        scratch_shapes=[s_buf, s_buf],
        compiler_params=_cparams(2),
        name="gqa_attention",
    )(qt, k, vt)


def _pivot_bcast(p, row, m, pivot):
    c, dk = p.shape
    blk = 2 * m
    p3 = p.reshape(c // SUBLANES, SUBLANES, dk)
    if blk == SUBLANES:
        return jnp.broadcast_to(p3[:, pivot:pivot + 1, :], p3.shape).reshape(c, dk)
    row3 = row.reshape(p3.shape)
    for j in range(int(math.log2(blk))):
        step = 1 << j
        if (pivot >> j) & 1:
            p3 = jnp.where((row3 & step) == 0, pltpu.roll(p3, SUBLANES - step, 1), p3)
        else:
            p3 = jnp.where((row3 & step) != 0, pltpu.roll(p3, step, 1), p3)
    return p3.reshape(c, dk)


_DONE = object()


def _hgrn_direction(q, g, v, lvl_ref, st_ref, o_ref, reverse):
    c = q.shape[0]
    f = jnp.exp2(g)
    k = 1.0 - f
    row = lax.broadcasted_iota(jnp.int32, q.shape, 0)
    p = g
    a = jnp.zeros((c, c), F32)
    for level in range(int(math.log2(c))):
        m = 1 << level
        if m < SUBLANES:
            qside = ((row & m) == 0) if reverse else ((row & m) != 0)
            r = _pivot_bcast(p, row, m, m if reverse else m - 1)
            if level == 0:
                y = jnp.where(qside, q * f, k)
            else:
                y = jnp.exp2(jnp.where(qside, p, r - p)) * jnp.where(qside, q, k)
            p = p + jnp.where(qside, r, 0.0)
        else:
            ys, ps = [], []
            for b in range(0, c, 2 * m):
                lo, hi = slice(b, b + m), slice(b + m, b + 2 * m)
                src, qry = (hi, lo) if reverse else (lo, hi)
                piv = b + m if reverse else b + m - 1
                r = p[piv:piv + 1, :]
                y_src = jnp.exp2(r - p[src]) * k[src]
                y_qry = jnp.exp2(p[qry]) * q[qry]
                p_qry = p[qry] + r
                ys += [y_qry, y_src] if reverse else [y_src, y_qry]
                ps += [p_qry, p[src]] if reverse else [p[src], p_qry]
            y = jnp.concatenate(ys, axis=0)
            p = jnp.concatenate(ps, axis=0)
        yb = y.astype(BF16)
        scores = _dot_nt(yb, yb)
        yield
        a = jnp.where(lvl_ref[...] == level, scores, a)
    diag = jnp.sum(q * k, axis=1, keepdims=True)
    st = st_ref[...]
    intra = _dot(a.astype(BF16), v)
    inter = _dot_nt((q * jnp.exp2(p)).astype(BF16), st.astype(BF16))
    tot = p[0:1, :] if reverse else p[c - 1:c, :]
    kd = k * jnp.exp2(tot - p)
    update = _dot_tn(v, kd.astype(BF16))
    yield
    o_ref[...] = intra + diag * v.astype(F32) + inter
    st_ref[...] = jnp.exp2(tot) * st + update


def _hgrn_kernel(qf_ref, gf_ref, vf_ref, qb_ref, gb_ref, vb_ref, lvlf_ref, lvlb_ref,
                 of_ref, ob_ref, st_ref):
    @pl.when(pl.program_id(1) == 0)
    def _():
        st_ref[...] = jnp.zeros_like(st_ref)

    chains = []
    for j in range(qf_ref.shape[0]):
        for h in range(HG_HEADS):
            sl = slice(h * HG_EXPAND, (h + 1) * HG_EXPAND)
            chains.append(_hgrn_direction(qf_ref[j, :, sl], gf_ref[j, :, sl], vf_ref[j, :, sl], lvlf_ref,
                                          st_ref.at[0, j, h], of_ref.at[j, :, sl], False))
            chains.append(_hgrn_direction(qb_ref[j, :, sl], gb_ref[j, :, sl], vb_ref[j, :, sl], lvlb_ref,
                                          st_ref.at[1, j, h], ob_ref.at[j, :, sl], True))
    while chains:
        chains = [ch for ch in chains if next(ch, _DONE) is not _DONE]


def _level_tables(c):
    t = np.arange(c)[:, None]
    s = np.arange(c)[None, :]
    x = t ^ s
    lv = np.where(x > 0, np.floor(np.log2(np.maximum(x, 1))), -1).astype(np.int32)
    fw = np.where(t > s, lv, -1).astype(np.int32)
    bw = np.where(t < s, lv, -1).astype(np.int32)
    return jnp.asarray(fw), jnp.asarray(bw)


def _hgrn(qh, gfw, gbw, ih):
    B, T, _ = qh.shape
    c = min(HG_CHUNK, T)
    nc = T // c
    seqs = min(HG_SEQS, B)
    lvl_fw, lvl_bw = _level_tables(c)
    blk = (seqs, c, HG_WIDTH)
    fw = pl.BlockSpec(blk, lambda b, i: (b, i, 0))
    bw = pl.BlockSpec(blk, lambda b, i: (b, nc - 1 - i, 0))
    out = jax.ShapeDtypeStruct(qh.shape, F32)
    return pl.pallas_call(
        _hgrn_kernel,
        grid=(B // seqs, nc),
        in_specs=[fw, fw, fw, bw, bw, bw, _const_spec((c, c)), _const_spec((c, c))],
        out_specs=[fw, bw],
        out_shape=[out, out],
        scratch_shapes=[pltpu.VMEM((2, seqs, HG_HEADS, HG_EXPAND, HG_EXPAND), F32)],
        compiler_params=_cparams(2),
        name="hgrn2_bidir",
    )(qh, gfw, ih, qh, gbw, ih, lvl_fw, lvl_bw)


def _merge_kernel(x_ref, oat_ref, of_ref, ob_ref, gh_ref, ga_ref, gb_ref,
                  wpa_ref, wpb_ref, wout_ref, gn_ref, lng_ref, lnb_ref, o_ref, *, alpha):
    o = of_ref[...] + ob_ref[...]
    gn = gn_ref[...]
    parts = []
    for h in range(HG_HEADS):
        oh = o[:, h * HG_EXPAND:(h + 1) * HG_EXPAND]
        ms = jnp.mean(oh * oh, axis=-1, keepdims=True)
        parts.append(oh * lax.rsqrt(ms + EPS) * gn)
    on = jnp.concatenate(parts, axis=1)
    o_b = (on * gh_ref[...].astype(F32)).astype(BF16)
    pa = _dot_tn(oat_ref[...], wpa_ref[...])
    pb = _dot(o_b, wpb_ref[...])
    merged = ga_ref[...].astype(F32) * pa + gb_ref[...].astype(F32) * pb
    y = _dot(merged.astype(BF16), wout_ref[...])
    o_ref[...] = _layer_norm(alpha * x_ref[...] + y, lng_ref[...], lnb_ref[...])


def _merge(x2d, o_at, o_fw, o_bw, gh, ga, gb, w_pa, w_pb, w_out, g_norm, ln_g, ln_b, alpha):
    n = x2d.shape[0]
    tm = ROW_TILE
    row = lambda w: pl.BlockSpec((tm, w), lambda i: (i, 0))
    vec = lambda a: a.astype(F32)[None, :]
    consts = [w_pa.astype(BF16), w_pb.astype(BF16), w_out.astype(BF16), vec(g_norm), vec(ln_g), vec(ln_b)]
    return pl.pallas_call(
        functools.partial(_merge_kernel, alpha=alpha),
        grid=(n // tm,),
        in_specs=[row(D_MODEL), pl.BlockSpec((ATTN_WIDTH, tm), lambda i: (0, i)),
                  row(HG_WIDTH), row(HG_WIDTH), row(HG_WIDTH),
                  row(D_MODEL), row(D_MODEL)] + [_const_spec(a.shape) for a in consts],
        out_specs=row(D_MODEL),
        out_shape=jax.ShapeDtypeStruct((n, D_MODEL), F32),
        compiler_params=_cparams(1),
        name="merge_ln1",
    )(x2d, o_at, o_fw, o_bw, gh, ga, gb, *consts)


def _memkv_kernel(m_ref, wk_ref, wv_ref, k_out, v_out):
    mb = m_ref[...].astype(BF16)
    k_out[...] = _dot(mb, wk_ref[...]).astype(k_out.dtype)
    v_out[...] = _dot(mb, wv_ref[...]).astype(v_out.dtype)


def _memkv(mem2d, w_k, w_v):
    n = mem2d.shape[0]
    tm = N_MEM
    row = pl.BlockSpec((tm, D_MODEL), lambda i: (i, 0))
    out = jax.ShapeDtypeStruct((n, D_MODEL), BF16)
    return pl.pallas_call(
        _memkv_kernel,
        grid=(n // tm,),
        in_specs=[row, _const_spec(w_k.shape), _const_spec(w_v.shape)],
        out_specs=[row, row],
        out_shape=[out, out],
        compiler_params=_cparams(1),
        name="mem_kv",
    )(mem2d, w_k.astype(BF16), w_v.astype(BF16))


def _xattn_kernel(x_ref, k_ref, v_ref, wq_ref, wo_ref, lng_ref, lnb_ref, o_ref, *, alpha):
    x = x_ref[...]
    scale = 1.0 / math.sqrt(X_HEAD_DIM)
    q = (_dot(x.astype(BF16), wq_ref[...]) * scale).astype(BF16)
    outs = []
    for h in range(X_HEADS):
        sl = slice(h * X_HEAD_DIM, (h + 1) * X_HEAD_DIM)
        s = _dot_nt(q[:, sl], k_ref[:, sl])
        p = jnp.exp(s - jnp.max(s, axis=-1, keepdims=True))
        l = jnp.sum(p, axis=-1, keepdims=True)
        outs.append(_dot(p.astype(BF16), v_ref[:, sl]) / l)
    o = jnp.concatenate(outs, axis=1).astype(BF16)
    y = _dot(o, wo_ref[...])
    o_ref[...] = _layer_norm(alpha * x + y, lng_ref[...], lnb_ref[...])


def _xattn(x2d, k_mem, v_mem, w_q, w_o, ln_g, ln_b, alpha, T):
    n = x2d.shape[0]
    tm = ROW_TILE
    tpb = T // tm
    row = pl.BlockSpec((tm, D_MODEL), lambda i: (i, 0))
    mem = pl.BlockSpec((N_MEM, D_MODEL), lambda i: (i // tpb, 0))
    vec = lambda a: a.astype(F32)[None, :]
    consts = [w_q.astype(BF16), w_o.astype(BF16), vec(ln_g), vec(ln_b)]
    return pl.pallas_call(
        functools.partial(_xattn_kernel, alpha=alpha),
        grid=(n // tm,),
        in_specs=[row, mem, mem] + [_const_spec(a.shape) for a in consts],
        out_specs=row,
        out_shape=jax.ShapeDtypeStruct((n, D_MODEL), F32),
        compiler_params=_cparams(1),
        name="mem_xattn_ln2",
    )(x2d, k_mem, v_mem, *consts)


def _mlp_kernel(x_ref, wu_ref, wd_ref, lng_ref, lnb_ref, o_ref, *, alpha):
    x = x_ref[...]
    h = jnp.maximum(_dot(x.astype(BF16), wu_ref[...]), 0.0)
    y = _dot((h * h).astype(BF16), wd_ref[...])
    o_ref[...] = _layer_norm(alpha * x + y, lng_ref[...], lnb_ref[...])


def _mlp(x2d, w_up, w_down, ln_g, ln_b, alpha):
    n = x2d.shape[0]
    tm = ROW_TILE
    row = pl.BlockSpec((tm, D_MODEL), lambda i: (i, 0))
    vec = lambda a: a.astype(F32)[None, :]
    consts = [w_up.astype(BF16), w_down.astype(BF16), vec(ln_g), vec(ln_b)]
    return pl.pallas_call(
        functools.partial(_mlp_kernel, alpha=alpha),
        grid=(n // tm,),
        in_specs=[row] + [_const_spec(a.shape) for a in consts],
        out_specs=row,
        out_shape=jax.ShapeDtypeStruct((n, D_MODEL), F32),
        compiler_params=_cparams(1),
        name="mlp_ln3",
    )(x2d, *consts)


def _run_group(x, mem, p, depth):
    B, T, _ = x.shape
    alpha = (2 * depth) ** 0.25
    x2d = x.reshape(B * T, D_MODEL)
    mem2d = mem.reshape(B * N_MEM, D_MODEL)
    cos, sin = _rope_tables(T)
    for l in range(depth):
        qt, k, vt, qh, gfw, gbw, ih, gh, ga, gb = _inproj(
            x2d, p["w_in"][l].astype(BF16), cos, sin, p["q_norm"][l], p["k_norm"][l], p["hg_lb"], l, T)
        o_at = _attention(qt, k, vt, T)
        o_fw, o_bw = _hgrn(*(a.reshape(B, T, HG_WIDTH) for a in (qh, gfw, gbw, ih)))
        x2d = _merge(x2d, o_at, o_fw.reshape(B * T, HG_WIDTH), o_bw.reshape(B * T, HG_WIDTH), gh, ga, gb,
                     p["w_pa"][l], p["w_pb"][l], p["w_out"][l],
                     p["hg_gnorm"][l], p["ln1_g"][l], p["ln1_b"][l], alpha)
        k_mem, v_mem = _memkv(mem2d, p["w_xk"][l], p["w_xv"][l])
        x2d = _xattn(x2d, k_mem, v_mem, p["w_xq"][l], p["w_xo"][l], p["ln2_g"][l], p["ln2_b"][l], alpha, T)
        x2d = _mlp(x2d, p["w_up"][l], p["w_down"][l], p["ln3_g"][l], p["ln3_b"][l], alpha)
    return x2d.reshape(B, T, D_MODEL)


def kernel(x_prompt, x_sample, mem_prompt, mem_sample, w_in, w_pa, w_pb, w_out, q_norm, k_norm, hg_lb, hg_gnorm, ln1_g, ln1_b, w_xq, w_xk, w_xv, w_xo, ln2_g, ln2_b, w_up, w_down, ln3_g, ln3_b):
    p = dict(w_in=w_in, w_pa=w_pa, w_pb=w_pb, w_out=w_out, q_norm=q_norm, k_norm=k_norm, hg_lb=hg_lb,
             hg_gnorm=hg_gnorm, ln1_g=ln1_g, ln1_b=ln1_b, w_xq=w_xq, w_xk=w_xk, w_xv=w_xv, w_xo=w_xo,
             ln2_g=ln2_g, ln2_b=ln2_b, w_up=w_up, w_down=w_down, ln3_g=ln3_g, ln3_b=ln3_b)
    depth = w_in.shape[0]
    return (_run_group(x_prompt, mem_prompt, p, depth), _run_group(x_sample, mem_sample, p, depth))
```

```python
import functools
import math

import numpy as np
import jax
import jax.numpy as jnp
from jax import lax
from jax.experimental import pallas as pl
from jax.experimental.pallas import tpu as pltpu

F32 = jnp.float32
BF16 = jnp.bfloat16

D_MODEL = 1024
GRID_W = 64
N_HEADS = 8
N_KV_HEADS = 2
HEAD_DIM = 64
ATTN_WIDTH = N_HEADS * HEAD_DIM
KV_WIDTH = N_KV_HEADS * HEAD_DIM
AXIS_DIM = HEAD_DIM // 2
ROPE_THETA = 10000.0
HG_HEADS = 4
HG_EXPAND = 128
HG_WIDTH = HG_HEADS * HG_EXPAND
N_MEM = 256
X_HEADS = 4
X_HEAD_DIM = D_MODEL // X_HEADS
D_FF = 4 * D_MODEL
EPS = 1e-6

LANES = 128
SUBLANES = 8
VMEM_LIMIT_V7X = 56 * 1024 * 1024

ROW_TILE = 512
ATTN_TQ = 256
ATTN_TK = 1024
HG_CHUNK = 128
HG_SEQS = 4
NEG_BIG = -1e30


def _cparams(n_axes):
    return pltpu.CompilerParams(
        dimension_semantics=("arbitrary",) * n_axes,
        vmem_limit_bytes=VMEM_LIMIT_V7X,
    )


def _const_spec(shape):
    nd = len(shape)
    return pl.BlockSpec(shape, lambda *_: (0,) * nd, pipeline_mode=pl.Buffered(1))


def _dot(a, b):
    return jnp.dot(a, b, preferred_element_type=F32)


def _dot_nt(a, b):
    return lax.dot_general(a, b, (((1,), (1,)), ((), ())), preferred_element_type=F32)


def _dot_tn(a, b):
    return lax.dot_general(a, b, (((0,), (0,)), ((), ())), preferred_element_type=F32)


def _sigmoid(x):
    return 0.5 * jnp.tanh(0.5 * x) + 0.5


def _layer_norm(z, g, b):
    mu = jnp.mean(z, axis=-1, keepdims=True)
    zc = z - mu
    var = jnp.mean(zc * zc, axis=-1, keepdims=True)
    return zc * lax.rsqrt(var + EPS) * g + b


_A_Q0 = 0
_A_K0 = _A_Q0 + ATTN_WIDTH
_A_HG0 = _A_K0 + 2 * KV_WIDTH
_A_GA0 = _A_HG0 + 5 * HG_WIDTH
_A_GB0 = _A_GA0 + D_MODEL
_NORM_CHUNK = 256
LOG2E = math.log2(math.e)


def _head_rmsnorm_rope(h, gain, cos, sin, bd, swap_lo):
    sq = h * h
    hi = sq.astype(BF16)
    lo = (sq - hi.astype(F32)).astype(BF16)
    ms = _dot(hi, bd) + _dot(lo, bd)
    hn = h * lax.rsqrt(ms + EPS) * gain
    n = h.shape[1]
    half = AXIS_DIM // 2
    partner = jnp.where(swap_lo, pltpu.roll(hn, n - half, 1), pltpu.roll(hn, half, 1))
    return hn * cos + partner * sin


def _inproj_kernel(x_ref, w_ref, cos_ref, sin_ref, qg_ref, kg_ref, bd_ref, lb_ref,
                   qt_out, k_out, vt_out, qh_out, gfw_out, gbw_out, ih_out, gh_out, ga_out, gb_out, *, layer):
    xb = x_ref[...].astype(BF16)
    cos = cos_ref[...]
    sin = sin_ref[...]
    bd = bd_ref[...]

    def swap_lo(n):
        lane = lax.broadcasted_iota(jnp.int32, (xb.shape[0], n), 1)
        return (lane & (AXIS_DIM - 1)) < (AXIS_DIM // 2)

    def proj(c0, n):
        return _dot(xb, w_ref[:, c0:c0 + n])

    scale = LOG2E / math.sqrt(HEAD_DIM)
    for c in range(ATTN_WIDTH // _NORM_CHUNK):
        h = proj(_A_Q0 + c * _NORM_CHUNK, _NORM_CHUNK)
        r = _head_rmsnorm_rope(h, qg_ref[...], cos, sin, bd, swap_lo(_NORM_CHUNK))
        qt_out[c * _NORM_CHUNK:(c + 1) * _NORM_CHUNK, :] = (r * scale).T.astype(qt_out.dtype)
    kw = KV_WIDTH
    kv = proj(_A_K0, 2 * kw)
    k_out[...] = _head_rmsnorm_rope(kv[:, :kw], kg_ref[...], cos[:, :kw], sin[:, :kw], bd[:kw, :kw],
                                    swap_lo(kw)).astype(k_out.dtype)
    vt_out[...] = kv[:, kw:].T.astype(vt_out.dtype)

    def hg(i):
        return proj(_A_HG0 + i * HG_WIDTH, HG_WIDTH)

    raw = lb_ref[...]
    e = jnp.exp(raw - jnp.max(raw, axis=1, keepdims=True))
    lbs = jnp.sum(e[:, :layer + 1, :], axis=1) / jnp.sum(e, axis=1)
    qh = hg(0)
    qh_out[...] = qh * _sigmoid(qh)
    for d, o in enumerate((gfw_out, gbw_out)):
        lb = lbs[d:d + 1, :]
        o[...] = jnp.log2(lb + (1.0 - lb) * _sigmoid(hg(1 + d)))
    ih_out[...] = hg(3).astype(ih_out.dtype)
    gh = hg(4)
    gh_out[...] = (gh * _sigmoid(gh)).astype(gh_out.dtype)
    ga_out[...] = _sigmoid(proj(_A_GA0, D_MODEL)).astype(ga_out.dtype)
    gb_out[...] = _sigmoid(proj(_A_GB0, D_MODEL)).astype(gb_out.dtype)


def _rope_tables(T):
    rows = T // GRID_W
    row = jnp.repeat(jnp.arange(rows, dtype=F32), GRID_W)
    col = jnp.tile(jnp.arange(GRID_W, dtype=F32), rows)
    inv_freq = ROPE_THETA ** (-jnp.arange(0, AXIS_DIM, 2, dtype=F32) / AXIS_DIM)
    ar = row[:, None] * inv_freq
    ac = col[:, None] * inv_freq
    cos = jnp.concatenate([jnp.cos(ar), jnp.cos(ar), jnp.cos(ac), jnp.cos(ac)], axis=1)
    sin = jnp.concatenate([-jnp.sin(ar), jnp.sin(ar), -jnp.sin(ac), jnp.sin(ac)], axis=1)
    reps = _NORM_CHUNK // HEAD_DIM
    return jnp.tile(cos, (1, reps)), jnp.tile(sin, (1, reps))


def _inproj(x2d, w_b, cos, sin, q_norm, k_norm, hg_lb, layer, T):
    n = x2d.shape[0]
    lb = hg_lb.astype(F32)
    tm = ROW_TILE
    tpb = T // tm
    reps = _NORM_CHUNK // HEAD_DIM
    qg = jnp.tile(q_norm.astype(F32), reps)[None, :]
    kg = jnp.tile(k_norm.astype(F32), KV_WIDTH // HEAD_DIM)[None, :]
    bd = jnp.asarray(np.kron(np.eye(reps, dtype=np.float32),
                             np.full((HEAD_DIM, HEAD_DIM), 1.0 / HEAD_DIM, np.float32)), BF16)
    row = lambda w: pl.BlockSpec((tm, w), lambda i: (i, 0))
    col = lambda h: pl.BlockSpec((h, tm), lambda i: (0, i))
    tab = pl.BlockSpec((tm, _NORM_CHUNK), lambda i: (i % tpb, 0))
    out = lambda w, dt: jax.ShapeDtypeStruct((n, w), dt)
    return pl.pallas_call(
        functools.partial(_inproj_kernel, layer=layer),
        grid=(n // tm,),
        in_specs=[row(D_MODEL), _const_spec(w_b.shape), tab, tab,
                  _const_spec(qg.shape), _const_spec(kg.shape), _const_spec(bd.shape), _const_spec(lb.shape)],
        out_specs=[col(ATTN_WIDTH), row(KV_WIDTH), col(KV_WIDTH)] + [row(HG_WIDTH)] * 5 + [row(D_MODEL)] * 2,
        out_shape=[jax.ShapeDtypeStruct((ATTN_WIDTH, n), BF16), jax.ShapeDtypeStruct((n, KV_WIDTH), BF16),
                   jax.ShapeDtypeStruct((KV_WIDTH, n), BF16)]
                  + [out(HG_WIDTH, dt) for dt in (F32, F32, F32, BF16, BF16)] + [out(D_MODEL, BF16)] * 2,
        compiler_params=_cparams(1),
        name="inproj",
    )(x2d, w_b, cos, sin, qg, kg, bd, lb)


_ATTN_STRAIGHT_LINE_BLOCKS = 4
_ONES_ROWS = 16


def _attn_kernel(qt_ref, k_ref, vt_ref, ot_ref, sa_ref, sb_ref, *, tk):
    tq = qt_ref.shape[1]
    nkb = k_ref.shape[0] // tk
    unroll = nkb if nkb <= _ATTN_STRAIGHT_LINE_BLOCKS else 2
    group = N_HEADS // N_KV_HEADS
    ones = jnp.ones((_ONES_ROWS, tk), BF16)
    zeros = jnp.zeros((HEAD_DIM, tq), BF16)
    together = sa_ref.shape[0]
    for h0 in range(0, N_HEADS, together):
        heads = range(h0, h0 + together)
        kv = h0 // group
        ws = []
        for h in heads:
            qh = qt_ref[h * HEAD_DIM:(h + 1) * HEAD_DIM, :]
            parts = [zeros] * N_KV_HEADS
            parts[kv] = qh
            ws.append(jnp.concatenate(parts, axis=0))

        def scores(kb, s_ref, ws=ws):
            kblk = k_ref[pl.ds(pl.multiple_of(kb * tk, tk), tk), :]
            for i, w in enumerate(ws):
                s_ref[i] = _dot(kblk, w)

        def process(kb, s_ref, carry, kv=kv):
            start = pl.multiple_of(kb * tk, tk)
            vext = jnp.concatenate([vt_ref[kv * HEAD_DIM:(kv + 1) * HEAD_DIM, pl.ds(start, tk)], ones], axis=0)
            out = []
            for i, (m, acc) in enumerate(carry):
                st = s_ref[i]
                m_new = jnp.maximum(m, jnp.max(st, axis=0, keepdims=True))
                alpha = jnp.exp2(m - m_new)
                pt = jnp.exp2(st - m_new).astype(BF16)
                out.append((m_new, acc * alpha + _dot(vext, pt)))
            return tuple(out)

        bufs = (sa_ref, sb_ref)

        def run_blocks(first, carry, last):
            for u in range(unroll):
                if not (last and u == unroll - 1):
                    scores(first + u + 1, bufs[(u + 1) % 2])
                carry = process(first + u, bufs[u % 2], carry)
            return carry

        init = tuple((jnp.full((1, tq), NEG_BIG, F32), jnp.zeros((HEAD_DIM + _ONES_ROWS, tq), F32))
                     for _ in heads)
        scores(0, sa_ref)
        carry = lax.fori_loop(0, nkb // unroll - 1, lambda i, c: run_blocks(i * unroll, c, False), init)
        carry = run_blocks(nkb - unroll, carry, True)
        for h, (_, acc) in zip(heads, carry):
            ot_ref[h * HEAD_DIM:(h + 1) * HEAD_DIM, :] = (
                acc[:HEAD_DIM] / acc[HEAD_DIM:HEAD_DIM + 1]).astype(ot_ref.dtype)


def _attention(qt, k, vt, T):
    n = k.shape[0]
    tq = min(ATTN_TQ, T)
    tk = min(ATTN_TK, T // 4)
    nq = T // tq
    assert (T // tk) % 2 == 0, "the two score buffers alternate"
    q_spec = pl.BlockSpec((ATTN_WIDTH, tq), lambda b, i: (0, b * nq + i))
    group = N_HEADS // N_KV_HEADS
    together = group if T // tk <= _ATTN_STRAIGHT_LINE_BLOCKS else group // 2
    s_buf = pltpu.VMEM((together, tk, tq), F32)
    return pl.pallas_call(
        functools.partial(_attn_kernel, tk=tk),
        grid=(n // T, nq),
        in_specs=[q_spec, pl.BlockSpec((T, KV_WIDTH), lambda b, i: (b, 0)),
                  pl.BlockSpec((KV_WIDTH, T), lambda b, i: (0, b))],
        out_specs=q_spec,
        out_shape=jax.ShapeDtypeStruct((ATTN_WIDTH, n), BF16),
        scratch_shapes=[s_buf, s_buf],
        compiler_params=_cparams(2),
        name="gqa_attention",
    )(qt, k, vt)


def _pivot_bcast(p, row, m, pivot):
    c, dk = p.shape
    blk = 2 * m
    p3 = p.reshape(c // SUBLANES, SUBLANES, dk)
    if blk == SUBLANES:
        return jnp.broadcast_to(p3[:, pivot:pivot + 1, :], p3.shape).reshape(c, dk)
    row3 = row.reshape(p3.shape)
    for j in range(int(math.log2(blk))):
        step = 1 << j
        if (pivot >> j) & 1:
            p3 = jnp.where((row3 & step) == 0, pltpu.roll(p3, SUBLANES - step, 1), p3)
        else:
            p3 = jnp.where((row3 & step) != 0, pltpu.roll(p3, step, 1), p3)
    return p3.reshape(c, dk)


_DONE = object()


def _hgrn_direction(q, g, v, lvl_ref, st_ref, o_ref, reverse):
    c = q.shape[0]
    f = jnp.exp2(g)
    k = 1.0 - f
    row = lax.broadcasted_iota(jnp.int32, q.shape, 0)
    p = g
    a = jnp.zeros((c, c), F32)
    for level in range(int(math.log2(c))):
        m = 1 << level
        if m < SUBLANES:
            qside = ((row & m) == 0) if reverse else ((row & m) != 0)
            r = _pivot_bcast(p, row, m, m if reverse else m - 1)
            if level == 0:
                y = jnp.where(qside, q * f, k)
            else:
                y = jnp.exp2(jnp.where(qside, p, r - p)) * jnp.where(qside, q, k)
            p = p + jnp.where(qside, r, 0.0)
        else:
            ys, ps = [], []
            for b in range(0, c, 2 * m):
                lo, hi = slice(b, b + m), slice(b + m, b + 2 * m)
                src, qry = (hi, lo) if reverse else (lo, hi)
                piv = b + m if reverse else b + m - 1
                r = p[piv:piv + 1, :]
                y_src = jnp.exp2(r - p[src]) * k[src]
                y_qry = jnp.exp2(p[qry]) * q[qry]
                p_qry = p[qry] + r
                ys += [y_qry, y_src] if reverse else [y_src, y_qry]
                ps += [p_qry, p[src]] if reverse else [p[src], p_qry]
            y = jnp.concatenate(ys, axis=0)
            p = jnp.concatenate(ps, axis=0)
        yb = y.astype(BF16)
        scores = _dot_nt(yb, yb)
        yield
        a = jnp.where(lvl_ref[...] == level, scores, a)
    diag = jnp.sum(q * k, axis=1, keepdims=True)
    st = st_ref[...]
    intra = _dot(a.astype(BF16), v)
    inter = _dot_nt((q * jnp.exp2(p)).astype(BF16), st.astype(BF16))
    tot = p[0:1, :] if reverse else p[c - 1:c, :]
    kd = k * jnp.exp2(tot - p)
    update = _dot_tn(v, kd.astype(BF16))
    yield
    o_ref[...] = intra + diag * v.astype(F32) + inter
    st_ref[...] = jnp.exp2(tot) * st + update


def _hgrn_kernel(qf_ref, gf_ref, vf_ref, qb_ref, gb_ref, vb_ref, lvlf_ref, lvlb_ref,
                 of_ref, ob_ref, st_ref):
    @pl.when(pl.program_id(1) == 0)
    def _():
        st_ref[...] = jnp.zeros_like(st_ref)

    chains = []
    for j in range(qf_ref.shape[0]):
        for h in range(HG_HEADS):
            sl = slice(h * HG_EXPAND, (h + 1) * HG_EXPAND)
            chains.append(_hgrn_direction(qf_ref[j, :, sl], gf_ref[j, :, sl], vf_ref[j, :, sl], lvlf_ref,
                                          st_ref.at[0, j, h], of_ref.at[j, :, sl], False))
            chains.append(_hgrn_direction(qb_ref[j, :, sl], gb_ref[j, :, sl], vb_ref[j, :, sl], lvlb_ref,
                                          st_ref.at[1, j, h], ob_ref.at[j, :, sl], True))
    while chains:
        chains = [ch for ch in chains if next(ch, _DONE) is not _DONE]


def _level_tables(c):
    t = np.arange(c)[:, None]
    s = np.arange(c)[None, :]
    x = t ^ s
    lv = np.where(x > 0, np.floor(np.log2(np.maximum(x, 1))), -1).astype(np.int32)
    fw = np.where(t > s, lv, -1).astype(np.int32)
    bw = np.where(t < s, lv, -1).astype(np.int32)
    return jnp.asarray(fw), jnp.asarray(bw)


def _hgrn(qh, gfw, gbw, ih):
    B, T, _ = qh.shape
    c = min(HG_CHUNK, T)
    nc = T // c
    seqs = min(HG_SEQS, B)
    lvl_fw, lvl_bw = _level_tables(c)
    blk = (seqs, c, HG_WIDTH)
    fw = pl.BlockSpec(blk, lambda b, i: (b, i, 0))
    bw = pl.BlockSpec(blk, lambda b, i: (b, nc - 1 - i, 0))
    out = jax.ShapeDtypeStruct(qh.shape, F32)
    return pl.pallas_call(
        _hgrn_kernel,
        grid=(B // seqs, nc),
        in_specs=[fw, fw, fw, bw, bw, bw, _const_spec((c, c)), _const_spec((c, c))],
        out_specs=[fw, bw],
        out_shape=[out, out],
        scratch_shapes=[pltpu.VMEM((2, seqs, HG_HEADS, HG_EXPAND, HG_EXPAND), F32)],
        compiler_params=_cparams(2),
        name="hgrn2_bidir",
    )(qh, gfw, ih, qh, gbw, ih, lvl_fw, lvl_bw)


def _merge_block(x_ref, oat_ref, of_ref, ob_ref, gh_ref, ga_ref, gb_ref,
                 wpa_ref, wpb_ref, wout_ref, gn_ref, lng_ref, lnb_ref, *, alpha):
    o = of_ref[...] + ob_ref[...]
    gn = gn_ref[...]
    parts = []
    for h in range(HG_HEADS):
        oh = o[:, h * HG_EXPAND:(h + 1) * HG_EXPAND]
        ms = jnp.mean(oh * oh, axis=-1, keepdims=True)
        parts.append(oh * lax.rsqrt(ms + EPS) * gn)
    on = jnp.concatenate(parts, axis=1)
    o_b = (on * gh_ref[...].astype(F32)).astype(BF16)
    pa = _dot_tn(oat_ref[...], wpa_ref[...])
    pb = _dot(o_b, wpb_ref[...])
    merged = ga_ref[...].astype(F32) * pa + gb_ref[...].astype(F32) * pb
    y = _dot(merged.astype(BF16), wout_ref[...])
    return _layer_norm(alpha * x_ref[...] + y, lng_ref[...], lnb_ref[...])


def _memkv_kernel(m_ref, wk_ref, wv_ref, k_out, v_out):
    mb = m_ref[...].astype(BF16)
    k_out[...] = _dot(mb, wk_ref[...]).astype(k_out.dtype)
    v_out[...] = _dot(mb, wv_ref[...]).astype(v_out.dtype)


def _memkv(mem2d, w_k, w_v):
    n = mem2d.shape[0]
    tm = N_MEM
    row = pl.BlockSpec((tm, D_MODEL), lambda i: (i, 0))
    out = jax.ShapeDtypeStruct((n, D_MODEL), BF16)
    return pl.pallas_call(
        _memkv_kernel,
        grid=(n // tm,),
        in_specs=[row, _const_spec(w_k.shape), _const_spec(w_v.shape)],
        out_specs=[row, row],
        out_shape=[out, out],
        compiler_params=_cparams(1),
        name="mem_kv",
    )(mem2d, w_k.astype(BF16), w_v.astype(BF16))


def _xattn_block(x, k_ref, v_ref, wq_ref, wo_ref, lng_ref, lnb_ref, alpha):
    scale = 1.0 / math.sqrt(X_HEAD_DIM)
    q = (_dot(x.astype(BF16), wq_ref[...]) * scale).astype(BF16)
    outs = []
    for h in range(X_HEADS):
        sl = slice(h * X_HEAD_DIM, (h + 1) * X_HEAD_DIM)
        s = _dot_nt(q[:, sl], k_ref[:, sl])
        p = jnp.exp(s - jnp.max(s, axis=-1, keepdims=True))
        l = jnp.sum(p, axis=-1, keepdims=True)
        outs.append(_dot(p.astype(BF16), v_ref[:, sl]) / l)
    o = jnp.concatenate(outs, axis=1).astype(BF16)
    y = _dot(o, wo_ref[...])
    return _layer_norm(alpha * x + y, lng_ref[...], lnb_ref[...])


_N_MERGE_REFS = 13


def _merge_xattn_kernel(*refs, alpha):
    x1 = _merge_block(*refs[:_N_MERGE_REFS], alpha=alpha)
    refs[-1][...] = _xattn_block(x1, *refs[_N_MERGE_REFS:-1], alpha)


def _merge_xattn(x2d, o_at, o_fw, o_bw, gh, ga, gb, w_pa, w_pb, w_out, g_norm, ln1_g, ln1_b,
                 k_mem, v_mem, w_q, w_o, ln2_g, ln2_b, alpha, T):
    n = x2d.shape[0]
    tm = ROW_TILE
    tpb = T // tm
    row = lambda w: pl.BlockSpec((tm, w), lambda i: (i, 0))
    mem = pl.BlockSpec((N_MEM, D_MODEL), lambda i: (i // tpb, 0))
    vec = lambda a: a.astype(F32)[None, :]
    consts1 = [w_pa.astype(BF16), w_pb.astype(BF16), w_out.astype(BF16), vec(g_norm), vec(ln1_g), vec(ln1_b)]
    consts2 = [w_q.astype(BF16), w_o.astype(BF16), vec(ln2_g), vec(ln2_b)]
    return pl.pallas_call(
        functools.partial(_merge_xattn_kernel, alpha=alpha),
        grid=(n // tm,),
        in_specs=[row(D_MODEL), pl.BlockSpec((ATTN_WIDTH, tm), lambda i: (0, i)),
                  row(HG_WIDTH), row(HG_WIDTH), row(HG_WIDTH), row(D_MODEL), row(D_MODEL)]
                 + [_const_spec(a.shape) for a in consts1] + [mem, mem] + [_const_spec(a.shape) for a in consts2],
        out_specs=row(D_MODEL),
        out_shape=jax.ShapeDtypeStruct((n, D_MODEL), F32),
        compiler_params=_cparams(1),
        name="merge_xattn",
    )(x2d, o_at, o_fw, o_bw, gh, ga, gb, *consts1, k_mem, v_mem, *consts2)


def _mlp_kernel(x_ref, wu_ref, wd_ref, lng_ref, lnb_ref, o_ref, *, alpha):
    x = x_ref[...]
    h = jnp.maximum(_dot(x.astype(BF16), wu_ref[...]), 0.0)
    y = _dot((h * h).astype(BF16), wd_ref[...])
    o_ref[...] = _layer_norm(alpha * x + y, lng_ref[...], lnb_ref[...])


def _mlp(x2d, w_up, w_down, ln_g, ln_b, alpha):
    n = x2d.shape[0]
    tm = ROW_TILE
    row = pl.BlockSpec((tm, D_MODEL), lambda i: (i, 0))
    vec = lambda a: a.astype(F32)[None, :]
    consts = [w_up.astype(BF16), w_down.astype(BF16), vec(ln_g), vec(ln_b)]
    return pl.pallas_call(
        functools.partial(_mlp_kernel, alpha=alpha),
        grid=(n // tm,),
        in_specs=[row] + [_const_spec(a.shape) for a in consts],
        out_specs=row,
        out_shape=jax.ShapeDtypeStruct((n, D_MODEL), F32),
        compiler_params=_cparams(1),
        name="mlp_ln3",
    )(x2d, *consts)


def _run_group(x, mem, p, depth):
    B, T, _ = x.shape
    alpha = (2 * depth) ** 0.25
    x2d = x.reshape(B * T, D_MODEL)
    mem2d = mem.reshape(B * N_MEM, D_MODEL)
    cos, sin = _rope_tables(T)
    for l in range(depth):
        qt, k, vt, qh, gfw, gbw, ih, gh, ga, gb = _inproj(
            x2d, p["w_in"][l].astype(BF16), cos, sin, p["q_norm"][l], p["k_norm"][l], p["hg_lb"], l, T)
        o_at = _attention(qt, k, vt, T)
        o_fw, o_bw = _hgrn(*(a.reshape(B, T, HG_WIDTH) for a in (qh, gfw, gbw, ih)))
        k_mem, v_mem = _memkv(mem2d, p["w_xk"][l], p["w_xv"][l])
        x2d = _merge_xattn(x2d, o_at, o_fw.reshape(B * T, HG_WIDTH), o_bw.reshape(B * T, HG_WIDTH), gh, ga, gb,
                           p["w_pa"][l], p["w_pb"][l], p["w_out"][l], p["hg_gnorm"][l], p["ln1_g"][l],
                           p["ln1_b"][l], k_mem, v_mem, p["w_xq"][l], p["w_xo"][l], p["ln2_g"][l],
                           p["ln2_b"][l], alpha, T)
        x2d = _mlp(x2d, p["w_up"][l], p["w_down"][l], p["ln3_g"][l], p["ln3_b"][l], alpha)
    return x2d.reshape(B, T, D_MODEL)


def kernel(x_prompt, x_sample, mem_prompt, mem_sample, w_in, w_pa, w_pb, w_out, q_norm, k_norm, hg_lb, hg_gnorm, ln1_g, ln1_b, w_xq, w_xk, w_xv, w_xo, ln2_g, ln2_b, w_up, w_down, ln3_g, ln3_b):
    p = dict(w_in=w_in, w_pa=w_pa, w_pb=w_pb, w_out=w_out, q_norm=q_norm, k_norm=k_norm, hg_lb=hg_lb,
             hg_gnorm=hg_gnorm, ln1_g=ln1_g, ln1_b=ln1_b, w_xq=w_xq, w_xk=w_xk, w_xv=w_xv, w_xo=w_xo,
             ln2_g=ln2_g, ln2_b=ln2_b, w_up=w_up, w_down=w_down, ln3_g=ln3_g, ln3_b=ln3_b)
    depth = w_in.shape[0]
    return (_run_group(x_prompt, mem_prompt, p, depth), _run_group(x_sample, mem_sample, p, depth))
```

```python
import functools
import math

import numpy as np
import jax
import jax.numpy as jnp
from jax import lax
from jax.experimental import pallas as pl
from jax.experimental.pallas import tpu as pltpu

F32 = jnp.float32
BF16 = jnp.bfloat16

D_MODEL = 1024
GRID_W = 64
N_HEADS = 8
N_KV_HEADS = 2
HEAD_DIM = 64
ATTN_WIDTH = N_HEADS * HEAD_DIM
KV_WIDTH = N_KV_HEADS * HEAD_DIM
AXIS_DIM = HEAD_DIM // 2
ROPE_THETA = 10000.0
HG_HEADS = 4
HG_EXPAND = 128
HG_WIDTH = HG_HEADS * HG_EXPAND
N_MEM = 256
X_HEADS = 4
X_HEAD_DIM = D_MODEL // X_HEADS
D_FF = 4 * D_MODEL
EPS = 1e-6

LANES = 128
SUBLANES = 8
VMEM_LIMIT_V7X = 56 * 1024 * 1024

ROW_TILE = 512
ATTN_TQ = 256
ATTN_TK = 1024
HG_CHUNK = 128
HG_SEQS = 4
NEG_BIG = -1e30


def _cparams(n_axes):
    return pltpu.CompilerParams(
        dimension_semantics=("arbitrary",) * n_axes,
        vmem_limit_bytes=VMEM_LIMIT_V7X,
    )


def _const_spec(shape):
    nd = len(shape)
    return pl.BlockSpec(shape, lambda *_: (0,) * nd, pipeline_mode=pl.Buffered(1))


def _dot(a, b):
    return jnp.dot(a, b, preferred_element_type=F32)


def _dot_nt(a, b):
    return lax.dot_general(a, b, (((1,), (1,)), ((), ())), preferred_element_type=F32)


def _dot_tn(a, b):
    return lax.dot_general(a, b, (((0,), (0,)), ((), ())), preferred_element_type=F32)


def _sigmoid(x):
    return 0.5 * jnp.tanh(0.5 * x) + 0.5


def _layer_norm(z, g, b):
    mu = jnp.mean(z, axis=-1, keepdims=True)
    zc = z - mu
    var = jnp.mean(zc * zc, axis=-1, keepdims=True)
    return zc * lax.rsqrt(var + EPS) * g + b


_A_Q0 = 0
_A_K0 = _A_Q0 + ATTN_WIDTH
_A_HG0 = _A_K0 + 2 * KV_WIDTH
_A_GA0 = _A_HG0 + 5 * HG_WIDTH
_A_GB0 = _A_GA0 + D_MODEL
_NORM_CHUNK = 256
LOG2E = math.log2(math.e)


def _head_rmsnorm_rope(h, gain, cos, sin, bd, swap_lo):
    sq = h * h
    hi = sq.astype(BF16)
    lo = (sq - hi.astype(F32)).astype(BF16)
    ms = _dot(hi, bd) + _dot(lo, bd)
    hn = h * lax.rsqrt(ms + EPS) * gain
    n = h.shape[1]
    half = AXIS_DIM // 2
    partner = jnp.where(swap_lo, pltpu.roll(hn, n - half, 1), pltpu.roll(hn, half, 1))
    return hn * cos + partner * sin


def _inproj_kernel(x_ref, w_ref, cos_ref, sin_ref, qg_ref, kg_ref, bd_ref, lb_ref,
                   qt_out, k_out, vt_out, qh_out, gfw_out, gbw_out, ih_out, gh_out, ga_out, gb_out, *, layer):
    xb = x_ref[...].astype(BF16)
    cos = cos_ref[...]
    sin = sin_ref[...]
    bd = bd_ref[...]

    def swap_lo(n):
        lane = lax.broadcasted_iota(jnp.int32, (xb.shape[0], n), 1)
        return (lane & (AXIS_DIM - 1)) < (AXIS_DIM // 2)

    def proj(c0, n):
        return _dot(xb, w_ref[:, c0:c0 + n])

    scale = LOG2E / math.sqrt(HEAD_DIM)
    kw = KV_WIDTH
    n_q = ATTN_WIDTH // _NORM_CHUNK
    hq = [proj(_A_Q0 + c * _NORM_CHUNK, _NORM_CHUNK) for c in range(n_q)]
    kv = proj(_A_K0, 2 * kw)

    def q_epilogue(c):
        r = _head_rmsnorm_rope(hq[c], qg_ref[...], cos, sin, bd, swap_lo(_NORM_CHUNK))
        qt_out[c * _NORM_CHUNK:(c + 1) * _NORM_CHUNK, :] = (r * scale).T.astype(qt_out.dtype)

    def kv_epilogue():
        k_out[...] = _head_rmsnorm_rope(kv[:, :kw], kg_ref[...], cos[:, :kw], sin[:, :kw], bd[:kw, :kw],
                                        swap_lo(kw)).astype(k_out.dtype)
        vt_out[...] = kv[:, kw:].T.astype(vt_out.dtype)

    norm_epilogues = [functools.partial(q_epilogue, c) for c in range(n_q)] + [kv_epilogue]

    def hg(i):
        return proj(_A_HG0 + i * HG_WIDTH, HG_WIDTH)

    raw = lb_ref[...]
    e = jnp.exp(raw - jnp.max(raw, axis=1, keepdims=True))
    lbs = jnp.sum(e[:, :layer + 1, :], axis=1) / jnp.sum(e, axis=1)
    qh = hg(0)
    qh_out[...] = qh * _sigmoid(qh)
    for d, o in enumerate((gfw_out, gbw_out)):
        lb = lbs[d:d + 1, :]
        z = hg(1 + d)
        norm_epilogues.pop(0)()
        o[...] = jnp.log2(lb + (1.0 - lb) * _sigmoid(z))
    ih = hg(3)
    norm_epilogues.pop(0)()
    ih_out[...] = ih.astype(ih_out.dtype)
    gh = hg(4)
    gh_out[...] = (gh * _sigmoid(gh)).astype(gh_out.dtype)
    ga_out[...] = _sigmoid(proj(_A_GA0, D_MODEL)).astype(ga_out.dtype)
    gb_out[...] = _sigmoid(proj(_A_GB0, D_MODEL)).astype(gb_out.dtype)


def _rope_tables(T):
    rows = T // GRID_W
    row = jnp.repeat(jnp.arange(rows, dtype=F32), GRID_W)
    col = jnp.tile(jnp.arange(GRID_W, dtype=F32), rows)
    inv_freq = ROPE_THETA ** (-jnp.arange(0, AXIS_DIM, 2, dtype=F32) / AXIS_DIM)
    ar = row[:, None] * inv_freq
    ac = col[:, None] * inv_freq
    cos = jnp.concatenate([jnp.cos(ar), jnp.cos(ar), jnp.cos(ac), jnp.cos(ac)], axis=1)
    sin = jnp.concatenate([-jnp.sin(ar), jnp.sin(ar), -jnp.sin(ac), jnp.sin(ac)], axis=1)
    reps = _NORM_CHUNK // HEAD_DIM
    return jnp.tile(cos, (1, reps)), jnp.tile(sin, (1, reps))


def _inproj(x2d, w_b, cos, sin, q_norm, k_norm, hg_lb, layer, T):
    n = x2d.shape[0]
    lb = hg_lb.astype(F32)
    tm = ROW_TILE
    tpb = T // tm
    reps = _NORM_CHUNK // HEAD_DIM
    qg = jnp.tile(q_norm.astype(F32), reps)[None, :]
    kg = jnp.tile(k_norm.astype(F32), KV_WIDTH // HEAD_DIM)[None, :]
    bd = jnp.asarray(np.kron(np.eye(reps, dtype=np.float32),
                             np.full((HEAD_DIM, HEAD_DIM), 1.0 / HEAD_DIM, np.float32)), BF16)
    row = lambda w: pl.BlockSpec((tm, w), lambda i: (i, 0))
    col = lambda h: pl.BlockSpec((h, tm), lambda i: (0, i))
    tab = pl.BlockSpec((tm, _NORM_CHUNK), lambda i: (i % tpb, 0))
    out = lambda w, dt: jax.ShapeDtypeStruct((n, w), dt)
    return pl.pallas_call(
        functools.partial(_inproj_kernel, layer=layer),
        grid=(n // tm,),
        in_specs=[row(D_MODEL), _const_spec(w_b.shape), tab, tab,
                  _const_spec(qg.shape), _const_spec(kg.shape), _const_spec(bd.shape), _const_spec(lb.shape)],
        out_specs=[col(ATTN_WIDTH), row(KV_WIDTH), col(KV_WIDTH)] + [row(HG_WIDTH)] * 5 + [row(D_MODEL)] * 2,
        out_shape=[jax.ShapeDtypeStruct((ATTN_WIDTH, n), BF16), jax.ShapeDtypeStruct((n, KV_WIDTH), BF16),
                   jax.ShapeDtypeStruct((KV_WIDTH, n), BF16)]
                  + [out(HG_WIDTH, dt) for dt in (F32, F32, F32, BF16, BF16)] + [out(D_MODEL, BF16)] * 2,
        compiler_params=_cparams(1),
        name="inproj",
    )(x2d, w_b, cos, sin, qg, kg, bd, lb)


_ATTN_STRAIGHT_LINE_BLOCKS = 4
_ONES_ROWS = 16


def _attn_kernel(qt_ref, k_ref, vt_ref, ot_ref, sa_ref, sb_ref, *, tk):
    tq = qt_ref.shape[1]
    nkb = k_ref.shape[0] // tk
    unroll = nkb if nkb <= _ATTN_STRAIGHT_LINE_BLOCKS else 2
    group = N_HEADS // N_KV_HEADS
    ones = jnp.ones((_ONES_ROWS, tk), BF16)
    zeros = jnp.zeros((HEAD_DIM, tq), BF16)
    together = sa_ref.shape[0]
    for h0 in range(0, N_HEADS, together):
        heads = range(h0, h0 + together)
        kv = h0 // group
        ws = []
        for h in heads:
            qh = qt_ref[h * HEAD_DIM:(h + 1) * HEAD_DIM, :]
            parts = [zeros] * N_KV_HEADS
            parts[kv] = qh
            ws.append(jnp.concatenate(parts, axis=0))

        def scores(kb, s_ref, ws=ws):
            kblk = k_ref[pl.ds(pl.multiple_of(kb * tk, tk), tk), :]
            for i, w in enumerate(ws):
                s_ref[i] = _dot(kblk, w)

        def process(kb, s_ref, carry, kv=kv):
            start = pl.multiple_of(kb * tk, tk)
            vext = jnp.concatenate([vt_ref[kv * HEAD_DIM:(kv + 1) * HEAD_DIM, pl.ds(start, tk)], ones], axis=0)
            out = []
            for i, (m, acc) in enumerate(carry):
                st = s_ref[i]
                m_new = jnp.maximum(m, jnp.max(st, axis=0, keepdims=True))
                alpha = jnp.exp2(m - m_new)
                pt = jnp.exp2(st - m_new).astype(BF16)
                out.append((m_new, acc * alpha + _dot(vext, pt)))
            return tuple(out)

        bufs = (sa_ref, sb_ref)

        def run_blocks(first, carry, last):
            for u in range(unroll):
                if not (last and u == unroll - 1):
                    scores(first + u + 1, bufs[(u + 1) % 2])
                carry = process(first + u, bufs[u % 2], carry)
            return carry

        init = tuple((jnp.full((1, tq), NEG_BIG, F32), jnp.zeros((HEAD_DIM + _ONES_ROWS, tq), F32))
                     for _ in heads)
        scores(0, sa_ref)
        carry = lax.fori_loop(0, nkb // unroll - 1, lambda i, c: run_blocks(i * unroll, c, False), init)
        carry = run_blocks(nkb - unroll, carry, True)
        for h, (_, acc) in zip(heads, carry):
            ot_ref[h * HEAD_DIM:(h + 1) * HEAD_DIM, :] = (
                acc[:HEAD_DIM] / acc[HEAD_DIM:HEAD_DIM + 1]).astype(ot_ref.dtype)


def _attention(qt, k, vt, T):
    n = k.shape[0]
    tq = min(ATTN_TQ, T)
    tk = min(ATTN_TK, T // 4)
    nq = T // tq
    assert (T // tk) % 2 == 0, "the two score buffers alternate"
    q_spec = pl.BlockSpec((ATTN_WIDTH, tq), lambda b, i: (0, b * nq + i))
    group = N_HEADS // N_KV_HEADS
    together = group if T // tk <= _ATTN_STRAIGHT_LINE_BLOCKS else group // 2
    s_buf = pltpu.VMEM((together, tk, tq), F32)
    return pl.pallas_call(
        functools.partial(_attn_kernel, tk=tk),
        grid=(n // T, nq),
        in_specs=[q_spec, pl.BlockSpec((T, KV_WIDTH), lambda b, i: (b, 0)),
                  pl.BlockSpec((KV_WIDTH, T), lambda b, i: (0, b))],
        out_specs=q_spec,
        out_shape=jax.ShapeDtypeStruct((ATTN_WIDTH, n), BF16),
        scratch_shapes=[s_buf, s_buf],
        compiler_params=_cparams(2),
        name="gqa_attention",
    )(qt, k, vt)


def _pivot_bcast(p, row, m, pivot):
    c, dk = p.shape
    blk = 2 * m
    p3 = p.reshape(c // SUBLANES, SUBLANES, dk)
    if blk == SUBLANES:
        return jnp.broadcast_to(p3[:, pivot:pivot + 1, :], p3.shape).reshape(c, dk)
    row3 = row.reshape(p3.shape)
    for j in range(int(math.log2(blk))):
        step = 1 << j
        if (pivot >> j) & 1:
            p3 = jnp.where((row3 & step) == 0, pltpu.roll(p3, SUBLANES - step, 1), p3)
        else:
            p3 = jnp.where((row3 & step) != 0, pltpu.roll(p3, step, 1), p3)
    return p3.reshape(c, dk)


_DONE = object()


def _hgrn_direction(q, g, v, lvl_ref, st_ref, o_ref, reverse):
    c = q.shape[0]
    f = jnp.exp2(g)
    k = 1.0 - f
    row = lax.broadcasted_iota(jnp.int32, q.shape, 0)
    p = g
    a = jnp.zeros((c, c), F32)
    for level in range(int(math.log2(c))):
        m = 1 << level
        if m < SUBLANES:
            qside = ((row & m) == 0) if reverse else ((row & m) != 0)
            r = _pivot_bcast(p, row, m, m if reverse else m - 1)
            if level == 0:
                y = jnp.where(qside, q * f, k)
            else:
                y = jnp.exp2(jnp.where(qside, p, r - p)) * jnp.where(qside, q, k)
            p = p + jnp.where(qside, r, 0.0)
        else:
            ys, ps = [], []
            for b in range(0, c, 2 * m):
                lo, hi = slice(b, b + m), slice(b + m, b + 2 * m)
                src, qry = (hi, lo) if reverse else (lo, hi)
                piv = b + m if reverse else b + m - 1
                r = p[piv:piv + 1, :]
                y_src = jnp.exp2(r - p[src]) * k[src]
                y_qry = jnp.exp2(p[qry]) * q[qry]
                p_qry = p[qry] + r
                ys += [y_qry, y_src] if reverse else [y_src, y_qry]
                ps += [p_qry, p[src]] if reverse else [p[src], p_qry]
            y = jnp.concatenate(ys, axis=0)
            p = jnp.concatenate(ps, axis=0)
        yb = y.astype(BF16)
        scores = _dot_nt(yb, yb)
        yield
        a = jnp.where(lvl_ref[...] == level, scores, a)
    diag = jnp.sum(q * k, axis=1, keepdims=True)
    st = st_ref[...]
    intra = _dot(a.astype(BF16), v)
    inter = _dot_nt((q * jnp.exp2(p)).astype(BF16), st.astype(BF16))
    tot = p[0:1, :] if reverse else p[c - 1:c, :]
    kd = k * jnp.exp2(tot - p)
    update = _dot_tn(v, kd.astype(BF16))
    yield
    o_ref[...] = intra + diag * v.astype(F32) + inter
    st_ref[...] = jnp.exp2(tot) * st + update


def _hgrn_kernel(qf_ref, gf_ref, vf_ref, qb_ref, gb_ref, vb_ref, lvlf_ref, lvlb_ref,
                 of_ref, ob_ref, st_ref):
    @pl.when(pl.program_id(1) == 0)
    def _():
        st_ref[...] = jnp.zeros_like(st_ref)

    chains = []
    for j in range(qf_ref.shape[0]):
        for h in range(HG_HEADS):
            sl = slice(h * HG_EXPAND, (h + 1) * HG_EXPAND)
            chains.append(_hgrn_direction(qf_ref[j, :, sl], gf_ref[j, :, sl], vf_ref[j, :, sl], lvlf_ref,
                                          st_ref.at[0, j, h], of_ref.at[j, :, sl], False))
            chains.append(_hgrn_direction(qb_ref[j, :, sl], gb_ref[j, :, sl], vb_ref[j, :, sl], lvlb_ref,
                                          st_ref.at[1, j, h], ob_ref.at[j, :, sl], True))
    while chains:
        chains = [ch for ch in chains if next(ch, _DONE) is not _DONE]


def _level_tables(c):
    t = np.arange(c)[:, None]
    s = np.arange(c)[None, :]
    x = t ^ s
    lv = np.where(x > 0, np.floor(np.log2(np.maximum(x, 1))), -1).astype(np.int32)
    fw = np.where(t > s, lv, -1).astype(np.int32)
    bw = np.where(t < s, lv, -1).astype(np.int32)
    return jnp.asarray(fw), jnp.asarray(bw)


def _hgrn(qh, gfw, gbw, ih):
    B, T, _ = qh.shape
    c = min(HG_CHUNK, T)
    nc = T // c
    seqs = min(HG_SEQS, B)
    lvl_fw, lvl_bw = _level_tables(c)
    blk = (seqs, c, HG_WIDTH)
    fw = pl.BlockSpec(blk, lambda b, i: (b, i, 0))
    bw = pl.BlockSpec(blk, lambda b, i: (b, nc - 1 - i, 0))
    out = jax.ShapeDtypeStruct(qh.shape, F32)
    return pl.pallas_call(
        _hgrn_kernel,
        grid=(B // seqs, nc),
        in_specs=[fw, fw, fw, bw, bw, bw, _const_spec((c, c)), _const_spec((c, c))],
        out_specs=[fw, bw],
        out_shape=[out, out],
        scratch_shapes=[pltpu.VMEM((2, seqs, HG_HEADS, HG_EXPAND, HG_EXPAND), F32)],
        compiler_params=_cparams(2),
        name="hgrn2_bidir",
    )(qh, gfw, ih, qh, gbw, ih, lvl_fw, lvl_bw)


def _merge_rows(rows, x_ref, oat_ref, of_ref, ob_ref, gh_ref, ga_ref, gb_ref,
                wpa_ref, wpb_ref, wout_ref, gn_ref, lng_ref, lnb_ref, *, alpha):
    o = of_ref[rows, :] + ob_ref[rows, :]
    gn = gn_ref[...]
    parts = []
    for h in range(HG_HEADS):
        oh = o[:, h * HG_EXPAND:(h + 1) * HG_EXPAND]
        ms = jnp.mean(oh * oh, axis=-1, keepdims=True)
        parts.append(oh * lax.rsqrt(ms + EPS) * gn)
    on = jnp.concatenate(parts, axis=1)
    o_b = (on * gh_ref[rows, :].astype(F32)).astype(BF16)
    pa = _dot_tn(oat_ref[:, rows], wpa_ref[...])
    pb = _dot(o_b, wpb_ref[...])
    yield
    merged = ga_ref[rows, :].astype(F32) * pa + gb_ref[rows, :].astype(F32) * pb
    y = _dot(merged.astype(BF16), wout_ref[...])
    yield
    return _layer_norm(alpha * x_ref[rows, :] + y, lng_ref[...], lnb_ref[...])


def _memkv_kernel(m_ref, wk_ref, wv_ref, k_out, v_out):
    mb = m_ref[...].astype(BF16)
    k_out[...] = _dot(mb, wk_ref[...]).astype(k_out.dtype)
    v_out[...] = _dot(mb, wv_ref[...]).astype(v_out.dtype)


def _memkv(mem2d, w_k, w_v):
    n = mem2d.shape[0]
    tm = N_MEM
    row = pl.BlockSpec((tm, D_MODEL), lambda i: (i, 0))
    out = jax.ShapeDtypeStruct((n, D_MODEL), BF16)
    return pl.pallas_call(
        _memkv_kernel,
        grid=(n // tm,),
        in_specs=[row, _const_spec(w_k.shape), _const_spec(w_v.shape)],
        out_specs=[row, row],
        out_shape=[out, out],
        compiler_params=_cparams(1),
        name="mem_kv",
    )(mem2d, w_k.astype(BF16), w_v.astype(BF16))


def _xattn_rows(x, k_ref, v_ref, wq_ref, wo_ref, lng_ref, lnb_ref, alpha):
    scale = 1.0 / math.sqrt(X_HEAD_DIM)
    qf = _dot(x.astype(BF16), wq_ref[...])
    yield
    q = (qf * scale).astype(BF16)
    heads = [slice(h * X_HEAD_DIM, (h + 1) * X_HEAD_DIM) for h in range(X_HEADS)]
    scores = [_dot_nt(q[:, sl], k_ref[:, sl]) for sl in heads]
    yield
    outs = []
    for s, sl in zip(scores, heads):
        p = jnp.exp(s - jnp.max(s, axis=-1, keepdims=True))
        l = jnp.sum(p, axis=-1, keepdims=True)
        outs.append((_dot(p.astype(BF16), v_ref[:, sl]), l))
    yield
    o = jnp.concatenate([pv / l for pv, l in outs], axis=1).astype(BF16)
    y = _dot(o, wo_ref[...])
    yield
    return _layer_norm(alpha * x + y, lng_ref[...], lnb_ref[...])


_N_MERGE_REFS = 13
_MX_SUBTILES = 2


def _merge_xattn_rows(rows, refs, alpha):
    x1 = yield from _merge_rows(rows, *refs[:_N_MERGE_REFS], alpha=alpha)
    refs[-1][rows, :] = yield from _xattn_rows(x1, *refs[_N_MERGE_REFS:-1], alpha)


def _merge_xattn_kernel(*refs, alpha):
    tm = refs[0].shape[0]
    sub = tm // _MX_SUBTILES
    chains = [_merge_xattn_rows(slice(i * sub, (i + 1) * sub), refs, alpha) for i in range(_MX_SUBTILES)]
    while chains:
        chains = [ch for ch in chains if next(ch, _DONE) is not _DONE]


def _merge_xattn(x2d, o_at, o_fw, o_bw, gh, ga, gb, w_pa, w_pb, w_out, g_norm, ln1_g, ln1_b,
                 k_mem, v_mem, w_q, w_o, ln2_g, ln2_b, alpha, T):
    n = x2d.shape[0]
    tm = ROW_TILE
    tpb = T // tm
    row = lambda w: pl.BlockSpec((tm, w), lambda i: (i, 0))
    mem = pl.BlockSpec((N_MEM, D_MODEL), lambda i: (i // tpb, 0))
    vec = lambda a: a.astype(F32)[None, :]
    consts1 = [w_pa.astype(BF16), w_pb.astype(BF16), w_out.astype(BF16), vec(g_norm), vec(ln1_g), vec(ln1_b)]
    consts2 = [w_q.astype(BF16), w_o.astype(BF16), vec(ln2_g), vec(ln2_b)]
    return pl.pallas_call(
        functools.partial(_merge_xattn_kernel, alpha=alpha),
        grid=(n // tm,),
        in_specs=[row(D_MODEL), pl.BlockSpec((ATTN_WIDTH, tm), lambda i: (0, i)),
                  row(HG_WIDTH), row(HG_WIDTH), row(HG_WIDTH), row(D_MODEL), row(D_MODEL)]
                 + [_const_spec(a.shape) for a in consts1] + [mem, mem] + [_const_spec(a.shape) for a in consts2],
        out_specs=row(D_MODEL),
        out_shape=jax.ShapeDtypeStruct((n, D_MODEL), F32),
        compiler_params=_cparams(1),
        name="merge_xattn",
    )(x2d, o_at, o_fw, o_bw, gh, ga, gb, *consts1, k_mem, v_mem, *consts2)


def _mlp_rows(rows, x_ref, wu_ref, wd_ref, lng_ref, lnb_ref, o_ref, alpha):
    x = x_ref[rows, :]
    up = _dot(x.astype(BF16), wu_ref[...])
    yield
    h = jnp.maximum(up, 0.0)
    y = _dot((h * h).astype(BF16), wd_ref[...])
    yield
    o_ref[rows, :] = _layer_norm(alpha * x + y, lng_ref[...], lnb_ref[...])


def _mlp_kernel(x_ref, *refs, alpha):
    sub = x_ref.shape[0] // _MX_SUBTILES
    chains = [_mlp_rows(slice(i * sub, (i + 1) * sub), x_ref, *refs, alpha) for i in range(_MX_SUBTILES)]
    while chains:
        chains = [ch for ch in chains if next(ch, _DONE) is not _DONE]


def _mlp(x2d, w_up, w_down, ln_g, ln_b, alpha):
    n = x2d.shape[0]
    tm = ROW_TILE
    row = pl.BlockSpec((tm, D_MODEL), lambda i: (i, 0))
    vec = lambda a: a.astype(F32)[None, :]
    consts = [w_up.astype(BF16), w_down.astype(BF16), vec(ln_g), vec(ln_b)]
    return pl.pallas_call(
        functools.partial(_mlp_kernel, alpha=alpha),
        grid=(n // tm,),
        in_specs=[row] + [_const_spec(a.shape) for a in consts],
        out_specs=row,
        out_shape=jax.ShapeDtypeStruct((n, D_MODEL), F32),
        compiler_params=_cparams(1),
        name="mlp_ln3",
    )(x2d, *consts)


def _run_group(x, mem, p, depth):
    B, T, _ = x.shape
    alpha = (2 * depth) ** 0.25
    x2d = x.reshape(B * T, D_MODEL)
    mem2d = mem.reshape(B * N_MEM, D_MODEL)
    cos, sin = _rope_tables(T)
    for l in range(depth):
        qt, k, vt, qh, gfw, gbw, ih, gh, ga, gb = _inproj(
            x2d, p["w_in"][l].astype(BF16), cos, sin, p["q_norm"][l], p["k_norm"][l], p["hg_lb"], l, T)
        o_at = _attention(qt, k, vt, T)
        o_fw, o_bw = _hgrn(*(a.reshape(B, T, HG_WIDTH) for a in (qh, gfw, gbw, ih)))
        k_mem, v_mem = _memkv(mem2d, p["w_xk"][l], p["w_xv"][l])
        x2d = _merge_xattn(x2d, o_at, o_fw.reshape(B * T, HG_WIDTH), o_bw.reshape(B * T, HG_WIDTH), gh, ga, gb,
                           p["w_pa"][l], p["w_pb"][l], p["w_out"][l], p["hg_gnorm"][l], p["ln1_g"][l],
                           p["ln1_b"][l], k_mem, v_mem, p["w_xq"][l], p["w_xo"][l], p["ln2_g"][l],
                           p["ln2_b"][l], alpha, T)
        x2d = _mlp(x2d, p["w_up"][l], p["w_down"][l], p["ln3_g"][l], p["ln3_b"][l], alpha)
    return x2d.reshape(B, T, D_MODEL)


def kernel(x_prompt, x_sample, mem_prompt, mem_sample, w_in, w_pa, w_pb, w_out, q_norm, k_norm, hg_lb, hg_gnorm, ln1_g, ln1_b, w_xq, w_xk, w_xv, w_xo, ln2_g, ln2_b, w_up, w_down, ln3_g, ln3_b):
    p = dict(w_in=w_in, w_pa=w_pa, w_pb=w_pb, w_out=w_out, q_norm=q_norm, k_norm=k_norm, hg_lb=hg_lb,
             hg_gnorm=hg_gnorm, ln1_g=ln1_g, ln1_b=ln1_b, w_xq=w_xq, w_xk=w_xk, w_xv=w_xv, w_xo=w_xo,
             ln2_g=ln2_g, ln2_b=ln2_b, w_up=w_up, w_down=w_down, ln3_g=ln3_g, ln3_b=ln3_b)
    depth = w_in.shape[0]
    return (_run_group(x_prompt, mem_prompt, p, depth), _run_group(x_sample, mem_sample, p, depth))
```

```python
import functools
import math

import numpy as np
import jax
import jax.numpy as jnp
from jax import lax
from jax.experimental import pallas as pl
from jax.experimental.pallas import tpu as pltpu

F32 = jnp.float32
BF16 = jnp.bfloat16

D_MODEL = 1024
GRID_W = 64
N_HEADS = 8
N_KV_HEADS = 2
HEAD_DIM = 64
ATTN_WIDTH = N_HEADS * HEAD_DIM
KV_WIDTH = N_KV_HEADS * HEAD_DIM
AXIS_DIM = HEAD_DIM // 2
ROPE_THETA = 10000.0
HG_HEADS = 4
HG_EXPAND = 128
HG_WIDTH = HG_HEADS * HG_EXPAND
N_MEM = 256
X_HEADS = 4
X_HEAD_DIM = D_MODEL // X_HEADS
D_FF = 4 * D_MODEL
EPS = 1e-6

LANES = 128
SUBLANES = 8
VMEM_LIMIT_V7X = 56 * 1024 * 1024

ROW_TILE = 512
ATTN_TQ = 256
ATTN_TK = 1024
HG_CHUNK = 128
HG_SEQS = 4
NEG_BIG = -1e30


def _cparams(n_axes):
    return pltpu.CompilerParams(
        dimension_semantics=("arbitrary",) * n_axes,
        vmem_limit_bytes=VMEM_LIMIT_V7X,
    )


def _const_spec(shape):
    nd = len(shape)
    return pl.BlockSpec(shape, lambda *_: (0,) * nd, pipeline_mode=pl.Buffered(1))


def _dot(a, b):
    return jnp.dot(a, b, preferred_element_type=F32)


def _dot_nt(a, b):
    return lax.dot_general(a, b, (((1,), (1,)), ((), ())), preferred_element_type=F32)


def _dot_tn(a, b):
    return lax.dot_general(a, b, (((0,), (0,)), ((), ())), preferred_element_type=F32)


def _sigmoid(x):
    return 0.5 * jnp.tanh(0.5 * x) + 0.5


def _layer_norm(z, g, b):
    mu = jnp.mean(z, axis=-1, keepdims=True)
    zc = z - mu
    var = jnp.mean(zc * zc, axis=-1, keepdims=True)
    return zc * lax.rsqrt(var + EPS) * g + b


_A_Q0 = 0
_A_K0 = _A_Q0 + ATTN_WIDTH
_A_HG0 = _A_K0 + 2 * KV_WIDTH
_A_GA0 = _A_HG0 + 5 * HG_WIDTH
_A_GB0 = _A_GA0 + D_MODEL
_NORM_CHUNK = 256
LOG2E = math.log2(math.e)


def _head_rmsnorm_rope(h, gain, cos, sin, bd, swap_lo):
    sq = h * h
    hi = sq.astype(BF16)
    lo = (sq - hi.astype(F32)).astype(BF16)
    ms = _dot(hi, bd) + _dot(lo, bd)
    hn = h * lax.rsqrt(ms + EPS) * gain
    n = h.shape[1]
    half = AXIS_DIM // 2
    partner = jnp.where(swap_lo, pltpu.roll(hn, n - half, 1), pltpu.roll(hn, half, 1))
    return hn * cos + partner * sin


def _inproj_kernel(x_ref, w_ref, cos_ref, sin_ref, qg_ref, kg_ref, bd_ref, lb_ref,
                   qt_out, k_out, vt_out, qh_out, gfw_out, gbw_out, ih_out, gh_out, ga_out, gb_out, *, layer):
    xb = x_ref[...].astype(BF16)
    cos = cos_ref[...]
    sin = sin_ref[...]
    bd = bd_ref[...]

    def swap_lo(n):
        lane = lax.broadcasted_iota(jnp.int32, (xb.shape[0], n), 1)
        return (lane & (AXIS_DIM - 1)) < (AXIS_DIM // 2)

    def proj(c0, n):
        return _dot(xb, w_ref[:, c0:c0 + n])

    scale = LOG2E / math.sqrt(HEAD_DIM)
    kw = KV_WIDTH
    n_q = ATTN_WIDTH // _NORM_CHUNK
    hq = [proj(_A_Q0 + c * _NORM_CHUNK, _NORM_CHUNK) for c in range(n_q)]
    kv = proj(_A_K0, 2 * kw)

    def q_epilogue(c):
        r = _head_rmsnorm_rope(hq[c], qg_ref[...], cos, sin, bd, swap_lo(_NORM_CHUNK))
        qt_out[c * _NORM_CHUNK:(c + 1) * _NORM_CHUNK, :] = (r * scale).T.astype(qt_out.dtype)

    def kv_epilogue():
        k_out[...] = _head_rmsnorm_rope(kv[:, :kw], kg_ref[...], cos[:, :kw], sin[:, :kw], bd[:kw, :kw],
                                        swap_lo(kw)).astype(k_out.dtype)
        vt_out[...] = kv[:, kw:].T.astype(vt_out.dtype)

    norm_epilogues = [functools.partial(q_epilogue, c) for c in range(n_q)] + [kv_epilogue]

    def hg(i):
        return proj(_A_HG0 + i * HG_WIDTH, HG_WIDTH)

    raw = lb_ref[...]
    e = jnp.exp(raw - jnp.max(raw, axis=1, keepdims=True))
    lbs = jnp.sum(e[:, :layer + 1, :], axis=1) / jnp.sum(e, axis=1)
    qh = hg(0)
    qh_out[...] = qh * _sigmoid(qh)
    for d, o in enumerate((gfw_out, gbw_out)):
        lb = lbs[d:d + 1, :]
        z = hg(1 + d)
        norm_epilogues.pop(0)()
        o[...] = jnp.log2(lb + (1.0 - lb) * _sigmoid(z))
    ih = hg(3)
    norm_epilogues.pop(0)()
    ih_out[...] = ih.astype(ih_out.dtype)
    gh = hg(4)
    gh_out[...] = (gh * _sigmoid(gh)).astype(gh_out.dtype)
    ga_out[...] = _sigmoid(proj(_A_GA0, D_MODEL)).astype(ga_out.dtype)
    gb_out[...] = _sigmoid(proj(_A_GB0, D_MODEL)).astype(gb_out.dtype)


def _rope_tables(T):
    rows = T // GRID_W
    row = jnp.repeat(jnp.arange(rows, dtype=F32), GRID_W)
    col = jnp.tile(jnp.arange(GRID_W, dtype=F32), rows)
    inv_freq = ROPE_THETA ** (-jnp.arange(0, AXIS_DIM, 2, dtype=F32) / AXIS_DIM)
    ar = row[:, None] * inv_freq
    ac = col[:, None] * inv_freq
    cos = jnp.concatenate([jnp.cos(ar), jnp.cos(ar), jnp.cos(ac), jnp.cos(ac)], axis=1)
    sin = jnp.concatenate([-jnp.sin(ar), jnp.sin(ar), -jnp.sin(ac), jnp.sin(ac)], axis=1)
    reps = _NORM_CHUNK // HEAD_DIM
    return jnp.tile(cos, (1, reps)), jnp.tile(sin, (1, reps))


def _inproj(x2d, w_b, cos, sin, q_norm, k_norm, hg_lb, layer, T):
    n = x2d.shape[0]
    lb = hg_lb.astype(F32)
    tm = ROW_TILE
    tpb = T // tm
    reps = _NORM_CHUNK // HEAD_DIM
    qg = jnp.tile(q_norm.astype(F32), reps)[None, :]
    kg = jnp.tile(k_norm.astype(F32), KV_WIDTH // HEAD_DIM)[None, :]
    bd = jnp.asarray(np.kron(np.eye(reps, dtype=np.float32),
                             np.full((HEAD_DIM, HEAD_DIM), 1.0 / HEAD_DIM, np.float32)), BF16)
    row = lambda w: pl.BlockSpec((tm, w), lambda i: (i, 0))
    col = lambda h: pl.BlockSpec((h, tm), lambda i: (0, i))
    tab = pl.BlockSpec((tm, _NORM_CHUNK), lambda i: (i % tpb, 0))
    out = lambda w, dt: jax.ShapeDtypeStruct((n, w), dt)
    return pl.pallas_call(
        functools.partial(_inproj_kernel, layer=layer),
        grid=(n // tm,),
        in_specs=[row(D_MODEL), _const_spec(w_b.shape), tab, tab,
                  _const_spec(qg.shape), _const_spec(kg.shape), _const_spec(bd.shape), _const_spec(lb.shape)],
        out_specs=[col(ATTN_WIDTH), row(KV_WIDTH), col(KV_WIDTH)] + [row(HG_WIDTH)] * 5 + [row(D_MODEL)] * 2,
        out_shape=[jax.ShapeDtypeStruct((ATTN_WIDTH, n), BF16), jax.ShapeDtypeStruct((n, KV_WIDTH), BF16),
                   jax.ShapeDtypeStruct((KV_WIDTH, n), BF16)]
                  + [out(HG_WIDTH, dt) for dt in (F32, F32, F32, BF16, BF16)] + [out(D_MODEL, BF16)] * 2,
        compiler_params=_cparams(1),
        name="inproj",
    )(x2d, w_b, cos, sin, qg, kg, bd, lb)


_ATTN_STRAIGHT_LINE_BLOCKS = 4
_ONES_ROWS = 16


def _attn_kernel(qt_ref, k_ref, vt_ref, ot_ref, sa_ref, sb_ref, *, tk):
    tq = qt_ref.shape[1]
    nkb = k_ref.shape[0] // tk
    unroll = nkb if nkb <= _ATTN_STRAIGHT_LINE_BLOCKS else 2
    group = N_HEADS // N_KV_HEADS
    ones = jnp.ones((_ONES_ROWS, tk), BF16)
    zeros = jnp.zeros((HEAD_DIM, tq), BF16)
    together = sa_ref.shape[0]

    def query_weights(h0):
        ws = []
        for h in range(h0, h0 + together):
            parts = [zeros] * N_KV_HEADS
            parts[h0 // group] = qt_ref[h * HEAD_DIM:(h + 1) * HEAD_DIM, :]
            ws.append(jnp.concatenate(parts, axis=0))
        return ws

    def scores_of(ws, kb, s_ref):
        kblk = k_ref[pl.ds(pl.multiple_of(kb * tk, tk), tk), :]
        for i, w in enumerate(ws):
            s_ref[i] = _dot(kblk, w)

    starts = list(range(0, N_HEADS, together))
    scores_of(query_weights(starts[0]), 0, sa_ref)
    for h0 in starts:
        heads = range(h0, h0 + together)
        kv = h0 // group
        scores = functools.partial(scores_of, query_weights(h0))
        nxt = h0 + together
        next_first = functools.partial(scores_of, query_weights(nxt), 0, sa_ref) if nxt < N_HEADS else None

        def process(kb, s_ref, carry, kv=kv):
            start = pl.multiple_of(kb * tk, tk)
            vext = jnp.concatenate([vt_ref[kv * HEAD_DIM:(kv + 1) * HEAD_DIM, pl.ds(start, tk)], ones], axis=0)
            out = []
            for i, (m, acc) in enumerate(carry):
                st = s_ref[i]
                m_new = jnp.maximum(m, jnp.max(st, axis=0, keepdims=True))
                alpha = jnp.exp2(m - m_new)
                pt = jnp.exp2(st - m_new).astype(BF16)
                out.append((m_new, acc * alpha + _dot(vext, pt)))
            return tuple(out)

        bufs = (sa_ref, sb_ref)

        def run_blocks(first, carry, last):
            for u in range(unroll):
                if not (last and u == unroll - 1):
                    scores(first + u + 1, bufs[(u + 1) % 2])
                elif next_first is not None:
                    next_first()
                carry = process(first + u, bufs[u % 2], carry)
            return carry

        init = tuple((jnp.full((1, tq), NEG_BIG, F32), jnp.zeros((HEAD_DIM + _ONES_ROWS, tq), F32))
                     for _ in heads)
        carry = lax.fori_loop(0, nkb // unroll - 1, lambda i, c: run_blocks(i * unroll, c, False), init)
        carry = run_blocks(nkb - unroll, carry, True)
        for h, (_, acc) in zip(heads, carry):
            ot_ref[h * HEAD_DIM:(h + 1) * HEAD_DIM, :] = (
                acc[:HEAD_DIM] / acc[HEAD_DIM:HEAD_DIM + 1]).astype(ot_ref.dtype)


def _attention(qt, k, vt, T):
    n = k.shape[0]
    tq = min(ATTN_TQ, T)
    tk = min(ATTN_TK, T // 4)
    nq = T // tq
    assert (T // tk) % 2 == 0, "the two score buffers alternate"
    q_spec = pl.BlockSpec((ATTN_WIDTH, tq), lambda b, i: (0, b * nq + i))
    group = N_HEADS // N_KV_HEADS
    together = group if T // tk <= _ATTN_STRAIGHT_LINE_BLOCKS else group // 2
    s_buf = pltpu.VMEM((together, tk, tq), F32)
    return pl.pallas_call(
        functools.partial(_attn_kernel, tk=tk),
        grid=(n // T, nq),
        in_specs=[q_spec, pl.BlockSpec((T, KV_WIDTH), lambda b, i: (b, 0)),
                  pl.BlockSpec((KV_WIDTH, T), lambda b, i: (0, b))],
        out_specs=q_spec,
        out_shape=jax.ShapeDtypeStruct((ATTN_WIDTH, n), BF16),
        scratch_shapes=[s_buf, s_buf],
        compiler_params=_cparams(2),
        name="gqa_attention",
    )(qt, k, vt)


def _pivot_bcast(p, row, m, pivot):
    c, dk = p.shape
    blk = 2 * m
    p3 = p.reshape(c // SUBLANES, SUBLANES, dk)
    if blk == SUBLANES:
        return jnp.broadcast_to(p3[:, pivot:pivot + 1, :], p3.shape).reshape(c, dk)
    row3 = row.reshape(p3.shape)
    for j in range(int(math.log2(blk))):
        step = 1 << j
        if (pivot >> j) & 1:
            p3 = jnp.where((row3 & step) == 0, pltpu.roll(p3, SUBLANES - step, 1), p3)
        else:
            p3 = jnp.where((row3 & step) != 0, pltpu.roll(p3, step, 1), p3)
    return p3.reshape(c, dk)


_DONE = object()


def _hgrn_direction(q, g, v, lvl_ref, st_ref, o_ref, reverse):
    c = q.shape[0]
    f = jnp.exp2(g)
    k = 1.0 - f
    row = lax.broadcasted_iota(jnp.int32, q.shape, 0)
    p = g
    a = jnp.zeros((c, c), F32)
    for level in range(int(math.log2(c))):
        m = 1 << level
        if m < SUBLANES:
            qside = ((row & m) == 0) if reverse else ((row & m) != 0)
            r = _pivot_bcast(p, row, m, m if reverse else m - 1)
            if level == 0:
                y = jnp.where(qside, q * f, k)
            else:
                y = jnp.exp2(jnp.where(qside, p, r - p)) * jnp.where(qside, q, k)
            p = p + jnp.where(qside, r, 0.0)
        else:
            ys, ps = [], []
            for b in range(0, c, 2 * m):
                lo, hi = slice(b, b + m), slice(b + m, b + 2 * m)
                src, qry = (hi, lo) if reverse else (lo, hi)
                piv = b + m if reverse else b + m - 1
                r = p[piv:piv + 1, :]
                y_src = jnp.exp2(r - p[src]) * k[src]
                y_qry = jnp.exp2(p[qry]) * q[qry]
                p_qry = p[qry] + r
                ys += [y_qry, y_src] if reverse else [y_src, y_qry]
                ps += [p_qry, p[src]] if reverse else [p[src], p_qry]
            y = jnp.concatenate(ys, axis=0)
            p = jnp.concatenate(ps, axis=0)
        yb = y.astype(BF16)
        scores = _dot_nt(yb, yb)
        yield
        a = jnp.where(lvl_ref[...] == level, scores, a)
    diag = jnp.sum(q * k, axis=1, keepdims=True)
    st = st_ref[...]
    intra = _dot(a.astype(BF16), v)
    inter = _dot_nt((q * jnp.exp2(p)).astype(BF16), st.astype(BF16))
    tot = p[0:1, :] if reverse else p[c - 1:c, :]
    kd = k * jnp.exp2(tot - p)
    update = _dot_tn(v, kd.astype(BF16))
    yield
    o_ref[...] = intra + diag * v.astype(F32) + inter
    st_ref[...] = jnp.exp2(tot) * st + update


def _hgrn_kernel(qf_ref, gf_ref, vf_ref, qb_ref, gb_ref, vb_ref, lvlf_ref, lvlb_ref,
                 of_ref, ob_ref, st_ref):
    @pl.when(pl.program_id(1) == 0)
    def _():
        st_ref[...] = jnp.zeros_like(st_ref)

    chains = []
    for j in range(qf_ref.shape[0]):
        for h in range(HG_HEADS):
            sl = slice(h * HG_EXPAND, (h + 1) * HG_EXPAND)
            chains.append(_hgrn_direction(qf_ref[j, :, sl], gf_ref[j, :, sl], vf_ref[j, :, sl], lvlf_ref,
                                          st_ref.at[0, j, h], of_ref.at[j, :, sl], False))
            chains.append(_hgrn_direction(qb_ref[j, :, sl], gb_ref[j, :, sl], vb_ref[j, :, sl], lvlb_ref,
                                          st_ref.at[1, j, h], ob_ref.at[j, :, sl], True))
    while chains:
        chains = [ch for ch in chains if next(ch, _DONE) is not _DONE]


def _level_tables(c):
    t = np.arange(c)[:, None]
    s = np.arange(c)[None, :]
    x = t ^ s
    lv = np.where(x > 0, np.floor(np.log2(np.maximum(x, 1))), -1).astype(np.int32)
    fw = np.where(t > s, lv, -1).astype(np.int32)
    bw = np.where(t < s, lv, -1).astype(np.int32)
    return jnp.asarray(fw), jnp.asarray(bw)


def _hgrn(qh, gfw, gbw, ih):
    B, T, _ = qh.shape
    c = min(HG_CHUNK, T)
    nc = T // c
    seqs = min(HG_SEQS, B)
    lvl_fw, lvl_bw = _level_tables(c)
    blk = (seqs, c, HG_WIDTH)
    fw = pl.BlockSpec(blk, lambda b, i: (b, i, 0))
    bw = pl.BlockSpec(blk, lambda b, i: (b, nc - 1 - i, 0))
    out = jax.ShapeDtypeStruct(qh.shape, F32)
    return pl.pallas_call(
        _hgrn_kernel,
        grid=(B // seqs, nc),
        in_specs=[fw, fw, fw, bw, bw, bw, _const_spec((c, c)), _const_spec((c, c))],
        out_specs=[fw, bw],
        out_shape=[out, out],
        scratch_shapes=[pltpu.VMEM((2, seqs, HG_HEADS, HG_EXPAND, HG_EXPAND), F32)],
        compiler_params=_cparams(2),
        name="hgrn2_bidir",
    )(qh, gfw, ih, qh, gbw, ih, lvl_fw, lvl_bw)


def _merge_rows(rows, x_ref, oat_ref, of_ref, ob_ref, gh_ref, ga_ref, gb_ref,
                wpa_ref, wpb_ref, wout_ref, gn_ref, lng_ref, lnb_ref, *, alpha):
    o = of_ref[rows, :] + ob_ref[rows, :]
    gn = gn_ref[...]
    parts = []
    for h in range(HG_HEADS):
        oh = o[:, h * HG_EXPAND:(h + 1) * HG_EXPAND]
        ms = jnp.mean(oh * oh, axis=-1, keepdims=True)
        parts.append(oh * lax.rsqrt(ms + EPS) * gn)
    on = jnp.concatenate(parts, axis=1)
    o_b = (on * gh_ref[rows, :].astype(F32)).astype(BF16)
    pa = _dot_tn(oat_ref[:, rows], wpa_ref[...])
    pb = _dot(o_b, wpb_ref[...])
    yield
    merged = ga_ref[rows, :].astype(F32) * pa + gb_ref[rows, :].astype(F32) * pb
    y = _dot(merged.astype(BF16), wout_ref[...])
    yield
    return _layer_norm(alpha * x_ref[rows, :] + y, lng_ref[...], lnb_ref[...])


def _memkv_kernel(m_ref, wk_ref, wv_ref, k_out, v_out):
    mb = m_ref[...].astype(BF16)
    k_out[...] = _dot(mb, wk_ref[...]).astype(k_out.dtype)
    v_out[...] = _dot(mb, wv_ref[...]).astype(v_out.dtype)


def _memkv(mem2d, w_k, w_v):
    n = mem2d.shape[0]
    tm = N_MEM
    row = pl.BlockSpec((tm, D_MODEL), lambda i: (i, 0))
    out = jax.ShapeDtypeStruct((n, D_MODEL), BF16)
    return pl.pallas_call(
        _memkv_kernel,
        grid=(n // tm,),
        in_specs=[row, _const_spec(w_k.shape), _const_spec(w_v.shape)],
        out_specs=[row, row],
        out_shape=[out, out],
        compiler_params=_cparams(1),
        name="mem_kv",
    )(mem2d, w_k.astype(BF16), w_v.astype(BF16))


def _xattn_rows(x, k_ref, v_ref, wq_ref, wo_ref, lng_ref, lnb_ref, alpha):
    scale = 1.0 / math.sqrt(X_HEAD_DIM)
    qf = _dot(x.astype(BF16), wq_ref[...])
    yield
    q = (qf * scale).astype(BF16)
    heads = [slice(h * X_HEAD_DIM, (h + 1) * X_HEAD_DIM) for h in range(X_HEADS)]
    scores = [_dot_nt(q[:, sl], k_ref[:, sl]) for sl in heads]
    yield
    outs = []
    for s, sl in zip(scores, heads):
        p = jnp.exp(s - jnp.max(s, axis=-1, keepdims=True))
        l = jnp.sum(p, axis=-1, keepdims=True)
        outs.append((_dot(p.astype(BF16), v_ref[:, sl]), l))
    yield
    o = jnp.concatenate([pv / l for pv, l in outs], axis=1).astype(BF16)
    y = _dot(o, wo_ref[...])
    yield
    return _layer_norm(alpha * x + y, lng_ref[...], lnb_ref[...])


_N_MERGE_REFS = 13
_MX_SUBTILES = 2


def _merge_xattn_rows(rows, refs, alpha):
    x1 = yield from _merge_rows(rows, *refs[:_N_MERGE_REFS], alpha=alpha)
    refs[-1][rows, :] = yield from _xattn_rows(x1, *refs[_N_MERGE_REFS:-1], alpha)


def _merge_xattn_kernel(*refs, alpha):
    tm = refs[0].shape[0]
    sub = tm // _MX_SUBTILES
    chains = [_merge_xattn_rows(slice(i * sub, (i + 1) * sub), refs, alpha) for i in range(_MX_SUBTILES)]
    while chains:
        chains = [ch for ch in chains if next(ch, _DONE) is not _DONE]


def _merge_xattn(x2d, o_at, o_fw, o_bw, gh, ga, gb, w_pa, w_pb, w_out, g_norm, ln1_g, ln1_b,
                 k_mem, v_mem, w_q, w_o, ln2_g, ln2_b, alpha, T):
    n = x2d.shape[0]
    tm = ROW_TILE
    tpb = T // tm
    row = lambda w: pl.BlockSpec((tm, w), lambda i: (i, 0))
    mem = pl.BlockSpec((N_MEM, D_MODEL), lambda i: (i // tpb, 0))
    vec = lambda a: a.astype(F32)[None, :]
    consts1 = [w_pa.astype(BF16), w_pb.astype(BF16), w_out.astype(BF16), vec(g_norm), vec(ln1_g), vec(ln1_b)]
    consts2 = [w_q.astype(BF16), w_o.astype(BF16), vec(ln2_g), vec(ln2_b)]
    return pl.pallas_call(
        functools.partial(_merge_xattn_kernel, alpha=alpha),
        grid=(n // tm,),
        in_specs=[row(D_MODEL), pl.BlockSpec((ATTN_WIDTH, tm), lambda i: (0, i)),
                  row(HG_WIDTH), row(HG_WIDTH), row(HG_WIDTH), row(D_MODEL), row(D_MODEL)]
                 + [_const_spec(a.shape) for a in consts1] + [mem, mem] + [_const_spec(a.shape) for a in consts2],
        out_specs=row(D_MODEL),
        out_shape=jax.ShapeDtypeStruct((n, D_MODEL), F32),
        compiler_params=_cparams(1),
        name="merge_xattn",
    )(x2d, o_at, o_fw, o_bw, gh, ga, gb, *consts1, k_mem, v_mem, *consts2)


def _mlp_rows(rows, x_ref, wu_ref, wd_ref, lng_ref, lnb_ref, o_ref, alpha):
    x = x_ref[rows, :]
    up = _dot(x.astype(BF16), wu_ref[...])
    yield
    h = jnp.maximum(up, 0.0)
    y = _dot((h * h).astype(BF16), wd_ref[...])
    yield
    o_ref[rows, :] = _layer_norm(alpha * x + y, lng_ref[...], lnb_ref[...])


def _mlp_kernel(x_ref, *refs, alpha):
    sub = x_ref.shape[0] // _MX_SUBTILES
    chains = [_mlp_rows(slice(i * sub, (i + 1) * sub), x_ref, *refs, alpha) for i in range(_MX_SUBTILES)]
    while chains:
        chains = [ch for ch in chains if next(ch, _DONE) is not _DONE]


def _mlp(x2d, w_up, w_down, ln_g, ln_b, alpha):
    n = x2d.shape[0]
    tm = ROW_TILE
    row = pl.BlockSpec((tm, D_MODEL), lambda i: (i, 0))
    vec = lambda a: a.astype(F32)[None, :]
    consts = [w_up.astype(BF16), w_down.astype(BF16), vec(ln_g), vec(ln_b)]
    return pl.pallas_call(
        functools.partial(_mlp_kernel, alpha=alpha),
        grid=(n // tm,),
        in_specs=[row] + [_const_spec(a.shape) for a in consts],
        out_specs=row,
        out_shape=jax.ShapeDtypeStruct((n, D_MODEL), F32),
        compiler_params=_cparams(1),
        name="mlp_ln3",
    )(x2d, *consts)


def _run_group(x, mem, p, depth):
    B, T, _ = x.shape
    alpha = (2 * depth) ** 0.25
    x2d = x.reshape(B * T, D_MODEL)
    mem2d = mem.reshape(B * N_MEM, D_MODEL)
    cos, sin = _rope_tables(T)
    for l in range(depth):
        qt, k, vt, qh, gfw, gbw, ih, gh, ga, gb = _inproj(
            x2d, p["w_in"][l].astype(BF16), cos, sin, p["q_norm"][l], p["k_norm"][l], p["hg_lb"], l, T)
        o_at = _attention(qt, k, vt, T)
        o_fw, o_bw = _hgrn(*(a.reshape(B, T, HG_WIDTH) for a in (qh, gfw, gbw, ih)))
        k_mem, v_mem = _memkv(mem2d, p["w_xk"][l], p["w_xv"][l])
        x2d = _merge_xattn(x2d, o_at, o_fw.reshape(B * T, HG_WIDTH), o_bw.reshape(B * T, HG_WIDTH), gh, ga, gb,
                           p["w_pa"][l], p["w_pb"][l], p["w_out"][l], p["hg_gnorm"][l], p["ln1_g"][l],
                           p["ln1_b"][l], k_mem, v_mem, p["w_xq"][l], p["w_xo"][l], p["ln2_g"][l],
                           p["ln2_b"][l], alpha, T)
        x2d = _mlp(x2d, p["w_up"][l], p["w_down"][l], p["ln3_g"][l], p["ln3_b"][l], alpha)
    return x2d.reshape(B, T, D_MODEL)


def kernel(x_prompt, x_sample, mem_prompt, mem_sample, w_in, w_pa, w_pb, w_out, q_norm, k_norm, hg_lb, hg_gnorm, ln1_g, ln1_b, w_xq, w_xk, w_xv, w_xo, ln2_g, ln2_b, w_up, w_down, ln3_g, ln3_b):
    p = dict(w_in=w_in, w_pa=w_pa, w_pb=w_pb, w_out=w_out, q_norm=q_norm, k_norm=k_norm, hg_lb=hg_lb,
             hg_gnorm=hg_gnorm, ln1_g=ln1_g, ln1_b=ln1_b, w_xq=w_xq, w_xk=w_xk, w_xv=w_xv, w_xo=w_xo,
             ln2_g=ln2_g, ln2_b=ln2_b, w_up=w_up, w_down=w_down, ln3_g=ln3_g, ln3_b=ln3_b)
    depth = w_in.shape[0]
    return (_run_group(x_prompt, mem_prompt, p, depth), _run_group(x_sample, mem_sample, p, depth))
```
